```python
import math
import jax
import jax.numpy as jnp
from jax import lax
import numpy as np

D_MODEL = 4096
BATCH = 2
SEQ = 8192
DEPTH = 2

F32 = jnp.float32
GRID_W = 64
CTX_LEN = 256
EPS = 1e-6
ATT_WIDTH = D_MODEL // 2
HEAD_DIM = 128
N_HEADS = ATT_WIDTH // HEAD_DIM
N_KV_HEADS = N_HEADS // 4
Q_PER_KV = N_HEADS // N_KV_HEADS
KV_WIDTH = N_KV_HEADS * HEAD_DIM
Q_BLOCK = 128
ROPE_THETA = 10000.0
SSM_WIDTH = D_MODEL // 2
SSM_GROUP = 16
SSM_GROUPS = SSM_WIDTH // SSM_GROUP
SSM_STATE = 64
SSM_CHUNK = 16
DT_MIN = 1e-3
DT_MAX = 1e-1
CONV_CH = D_MODEL
CONV_K = 3
EVEN_PARTS = (("q", ATT_WIDTH), ("k", KV_WIDTH), ("v", KV_WIDTH), ("g_att", ATT_WIDTH), ("u", SSM_WIDTH), ("g_ssm", SSM_WIDTH))
EVEN_IN = 2 * ATT_WIDTH + 2 * KV_WIDTH + 2 * SSM_WIDTH
ODD_IN = 4 * CONV_CH

kernel_name = "hybrid_gqa_s5_shortconv_dit"


def rms_norm(x, w):
    xf = x.astype(F32)
    y = xf * lax.rsqrt(jnp.mean(xf * xf, axis=-1, keepdims=True) + EPS)
    return (y * w.astype(F32)).astype(x.dtype)


def ada_mod(cond, w_mod, b_mod):
    m = (jax.nn.silu(cond) @ w_mod + b_mod)[..., None, :]
    return jnp.split(m, 3, axis=-1)


def modulate(h, norm_w, shift, scale):
    return rms_norm(h, norm_w) * (1 + scale) + shift


def even_part(t, name):
    off = 0
    for part, width in EVEN_PARTS:
        if part == name:
            return t[..., off:off + width]
        off += width
    raise KeyError(name)


def axial_rope_tables(n_tok):
    rows = n_tok // GRID_W
    r, col = jnp.meshgrid(jnp.arange(rows, dtype=F32), jnp.arange(GRID_W, dtype=F32), indexing="ij")
    axis_dim = HEAD_DIM // 2
    inv_freq = ROPE_THETA ** (-jnp.arange(0, axis_dim, 2, dtype=F32) / axis_dim)
    ang = jnp.stack([r.reshape(-1)[:, None] * inv_freq, col.reshape(-1)[:, None] * inv_freq], axis=1)
    return jnp.cos(ang), jnp.sin(ang)


def apply_axial_rope(t, cos, sin):
    xs = t.astype(F32).reshape(t.shape[:-1] + (2, 2, HEAD_DIM // 4))
    x1, x2 = xs[..., 0, :], xs[..., 1, :]
    cs, sn = cos[None, :, None], sin[None, :, None]
    out = jnp.stack([x1 * cs - x2 * sn, x2 * cs + x1 * sn], axis=-2)
    return out.reshape(t.shape).astype(t.dtype)


def _attend(qg, k, v):
    s = jnp.einsum("bqkgd,bskd->bkgqs", qg, k, preferred_element_type=F32) * (HEAD_DIM ** -0.5)
    p = jax.nn.softmax(s, axis=-1).astype(v.dtype)
    return jnp.einsum("bkgqs,bskd->bqkgd", p, v)


def latent_attention(q, k, v, k_ctx, v_ctx):
    b, n = q.shape[0], q.shape[1]
    k_all = jnp.concatenate([k_ctx, k], axis=1)
    v_all = jnp.concatenate([v_ctx, v], axis=1)
    qb = q.reshape(b, n // Q_BLOCK, Q_BLOCK, N_KV_HEADS, Q_PER_KV, HEAD_DIM).swapaxes(0, 1)
    o = lax.map(lambda blk: _attend(blk, k_all, v_all), qb)
    return o.swapaxes(0, 1).reshape(b, n, ATT_WIDTH)


def s5_discretize(lam_re, lam_im, log_dt, b_re, b_im):
    dt = jnp.exp(log_dt.astype(F32))[:, None]
    lr, li = lam_re.astype(F32), lam_im.astype(F32)
    mag = jnp.exp(lr * dt)
    a_re, a_im = mag * jnp.cos(li * dt), mag * jnp.sin(li * dt)
    den = lr * lr + li * li
    n_re = a_re - 1.0
    coef_re = ((n_re * lr + a_im * li) / den)[..., None]
    coef_im = ((a_im * lr - n_re * li) / den)[..., None]
    br, bi = b_re.astype(F32), b_im.astype(F32)
    return (a_re, a_im, coef_re * br - coef_im * bi, coef_re * bi + coef_im * br)


def _ssm_combine(e1, e2):
    a1r, a1i, b1r, b1i = e1
    a2r, a2i, b2r, b2i = e2
    return (a1r * a2r - a1i * a2i, a1r * a2i + a1i * a2r,
            a2r * b1r - a2i * b1i + b2r, a2r * b1i + a2i * b1r + b2i)


def s5_scan(u, disc, h0, reverse):
    a_re, a_im, bb_re, bb_im = disc
    bu_re = jnp.einsum("blgi,gpi->blgp", u, bb_re)
    bu_im = jnp.einsum("blgi,gpi->blgp", u, bb_im)
    if h0 is not None:
        h0_re, h0_im = h0
        first = -1 if reverse else 0
        bu_re = bu_re.at[:, first].add(a_re * h0_re - a_im * h0_im)
        bu_im = bu_im.at[:, first].add(a_re * h0_im + a_im * h0_re)
    elems = (jnp.broadcast_to(a_re, bu_re.shape), jnp.broadcast_to(a_im, bu_im.shape), bu_re, bu_im)
    _, _, h_re, h_im = lax.associative_scan(_ssm_combine, elems, reverse=reverse, axis=1)
    return h_re, h_im


def s5_readout(h_re, h_im, c_re, c_im):
    return jnp.einsum("blgp,gip->blgi", h_re, c_re) - jnp.einsum("blgp,gip->blgi", h_im, c_im)


def _chunk_groups(t, axis):
    n_chunks = SSM_GROUPS // SSM_CHUNK
    t = t.reshape(t.shape[:axis] + (n_chunks, SSM_CHUNK) + t.shape[axis + 1:])
    return jnp.moveaxis(t, axis, 0)


def s5_latent(u, fwd, bwd, h0_fwd, h0_bwd):
    b, n = u.shape[0], u.shape[1]
    xs = (_chunk_groups(u, 2),
          tuple(_chunk_groups(t, 1) for t in h0_fwd), tuple(_chunk_groups(t, 1) for t in h0_bwd),
          tuple(_chunk_groups(t, 0) for t in fwd), tuple(_chunk_groups(t, 0) for t in bwd))

    def chunk(args):
        uc, hf, hb, pf, pb = args
        y = s5_readout(*s5_scan(uc, pf[:4], hf, False), *pf[4:])
        return y + s5_readout(*s5_scan(uc, pb[:4], hb, True), *pb[4:])

    y = lax.map(chunk, xs)
    return jnp.moveaxis(y, 0, 2).reshape(b, n, SSM_WIDTH)


def ssm_glu(y, w_glu, b_glu):
    y = jax.nn.gelu(y)
    return y * jax.nn.sigmoid(y @ w_glu + b_glu)


def attn_ssm_layer(h, hc, p, c, c_ctx, cos, sin, ctx_out):
    b, n, _ = h.shape
    nc = hc.shape[1]
    shift, scale, gate = ada_mod(c, p["w_mod"], p["b_mod"])
    shift_c, scale_c, gate_c = ada_mod(c_ctx, p["w_mod"], p["b_mod"])
    xn = modulate(h, p["norm_w"], shift, scale)
    xcn = modulate(hc, p["norm_w"], shift_c, scale_c)
    w_in = p["w_in"]
    z = xn @ w_in

    q = apply_axial_rope(rms_norm(even_part(z, "q").reshape(b, n, N_HEADS, HEAD_DIM), p["q_norm_w"]), cos, sin)
    k = apply_axial_rope(rms_norm(even_part(z, "k").reshape(b, n, N_KV_HEADS, HEAD_DIM), p["k_norm_w"]), cos, sin)
    v = even_part(z, "v").reshape(b, n, N_KV_HEADS, HEAD_DIM)
    kc = rms_norm((xcn @ even_part(w_in, "k")).reshape(b, nc, N_KV_HEADS, HEAD_DIM), p["k_norm_w"])
    vc = (xcn @ even_part(w_in, "v")).reshape(b, nc, N_KV_HEADS, HEAD_DIM)
    att = latent_attention(q, k, v, kc, vc)

    fwd = s5_discretize(*p["fwd"][:5]) + (p["fwd"][5].astype(F32), p["fwd"][6].astype(F32))
    bwd = s5_discretize(*p["bwd"][:5]) + (p["bwd"][5].astype(F32), p["bwd"][6].astype(F32))
    d_skip = p["ssm_d"].astype(F32)
    u = even_part(z, "u").astype(F32)
    uc = (xcn @ even_part(w_in, "u")).astype(F32)
    uc_g = uc.reshape(b, nc, SSM_GROUPS, SSM_GROUP)
    hcf_re, hcf_im = s5_scan(uc_g, fwd[:4], None, False)
    hcb_re, hcb_im = s5_scan(uc_g, bwd[:4], None, True)
    y = s5_latent(u.reshape(b, n, SSM_GROUPS, SSM_GROUP), fwd, bwd,
                  (hcf_re[:, -1], hcf_im[:, -1]), (hcb_re[:, 0], hcb_im[:, 0]))
    ssm = ssm_glu((y + d_skip * u).astype(h.dtype), p["w_glu"], p["b_glu"])

    mixed = jnp.concatenate([att * jax.nn.silu(even_part(z, "g_att")),
                             ssm * jax.nn.silu(even_part(z, "g_ssm"))], axis=-1)
    h_new = h + gate * (mixed @ p["w_out"])

    if ctx_out:
        qc = rms_norm((xcn @ even_part(w_in, "q")).reshape(b, nc, N_HEADS, HEAD_DIM), p["q_norm_w"])
        att_c = _attend(qc.reshape(b, nc, N_KV_HEADS, Q_PER_KV, HEAD_DIM), kc, vc).reshape(b, nc, ATT_WIDTH)
        yc = (s5_readout(hcf_re, hcf_im, *fwd[4:]) + s5_readout(hcb_re, hcb_im, *bwd[4:])).reshape(b, nc, SSM_WIDTH)
        ssm_c = ssm_glu((yc + d_skip * uc).astype(hc.dtype), p["w_glu"], p["b_glu"])
        mixed_c = jnp.concatenate([att_c * jax.nn.silu(xcn @ even_part(w_in, "g_att")),
                                   ssm_c * jax.nn.silu(xcn @ even_part(w_in, "g_ssm"))], axis=-1)
        hc = hc + gate_c * (mixed_c @ p["w_out"])
    return h_new, hc


def short_conv_mixer(xn, w_in, conv_w, conv_b, w_out):
    n = xn.shape[1]
    b_gate, c_gate, xin, g = jnp.split(xn @ w_in, 4, axis=-1)
    y = c_gate * xin
    pad = CONV_K // 2
    yp = jnp.pad(y, ((0, 0), (pad, pad), (0, 0)))
    conv = conv_b
    for j in range(CONV_K):
        conv = conv + yp[:, j:j + n] * conv_w[j]
    return (b_gate * conv * jax.nn.silu(g)) @ w_out


def shortconv_layer(h, hc, p, c, c_ctx, ctx_out):
    shift, scale, gate = ada_mod(c, p["w_mod"], p["b_mod"])
    xn = modulate(h, p["norm_w"], shift, scale)
    h_new = h + gate * short_conv_mixer(xn, p["w_in"], p["conv_w"], p["conv_b"], p["w_out"])
    if ctx_out:
        shift_c, scale_c, gate_c = ada_mod(c_ctx, p["w_mod"], p["b_mod"])
        xcn = modulate(hc, p["norm_w"], shift_c, scale_c)
        hc = hc + gate_c * short_conv_mixer(xcn, p["w_in"], p["conv_w"], p["conv_b"], p["w_out"])
    return h_new, hc


def setup_inputs(seed: int = 0) -> dict:
    key = jax.random.key(seed)
    ks = iter(jax.random.split(key, 64))

    def nrm(shape, s):
        return jax.random.normal(next(ks), shape, F32) * s

    d = D_MODEL
    g, pst, ci = SSM_GROUPS, SSM_STATE, SSM_GROUP

    def ssm_dir():
        n_idx = jnp.arange(pst, dtype=F32)[None, :]
        lam_re = -0.5 + nrm((g, pst), 0.01)
        lam_im = math.pi * n_idx + nrm((g, pst), 0.01)
        log_dt = jax.random.uniform(next(ks), (g,), F32, math.log(DT_MIN), math.log(DT_MAX))
        b_re = nrm((g, pst, ci), (2.0 * ci) ** -0.5)
        b_im = nrm((g, pst, ci), (2.0 * ci) ** -0.5)
        c_re = nrm((g, ci, pst), (2.0 * pst) ** -0.5)
        c_im = nrm((g, ci, pst), (2.0 * pst) ** -0.5)
        return lam_re, lam_im, log_dt, b_re, b_im, c_re, c_im

    inp = {}
    inp["x"] = nrm((BATCH, SEQ, d), 1.0)
    inp["c"] = nrm((BATCH, d), 1.0)
    inp["ctx"] = nrm((BATCH, CTX_LEN, d), 1.0)
    inp["c_ctx"] = nrm((d,), 1.0)
    inp["l0_norm_w"] = 1.0 + nrm((d,), 0.02)
    inp["l0_w_mod"] = nrm((d, 3 * d), 0.5 * d ** -0.5)
    inp["l0_b_mod"] = nrm((3 * d,), 0.01)
    inp["l0_w_in"] = nrm((d, EVEN_IN), d ** -0.5)
    inp["l0_q_norm_w"] = 1.0 + nrm((HEAD_DIM,), 0.02)
    inp["l0_k_norm_w"] = 1.0 + nrm((HEAD_DIM,), 0.02)
    (inp["l0_fwd_lam_re"], inp["l0_fwd_lam_im"], inp["l0_fwd_log_dt"], inp["l0_fwd_b_re"],
     inp["l0_fwd_b_im"], inp["l0_fwd_c_re"], inp["l0_fwd_c_im"]) = ssm_dir()
    (inp["l0_bwd_lam_re"], inp["l0_bwd_lam_im"], inp["l0_bwd_log_dt"], inp["l0_bwd_b_re"],
     inp["l0_bwd_b_im"], inp["l0_bwd_c_re"], inp["l0_bwd_c_im"]) = ssm_dir()
    inp["l0_ssm_d"] = nrm((SSM_WIDTH,), 1.0)
    inp["l0_w_glu"] = nrm((SSM_WIDTH, SSM_WIDTH), SSM_WIDTH ** -0.5)
    inp["l0_b_glu"] = nrm((SSM_WIDTH,), 0.01)
    inp["l0_w_out"] = nrm((ATT_WIDTH + SSM_WIDTH, d), (ATT_WIDTH + SSM_WIDTH) ** -0.5)
    inp["l1_norm_w"] = 1.0 + nrm((d,), 0.02)
    inp["l1_w_mod"] = nrm((d, 3 * d), 0.5 * d ** -0.5)
    inp["l1_b_mod"] = nrm((3 * d,), 0.01)
    inp["l1_w_in"] = nrm((d, ODD_IN), d ** -0.5)
    inp["l1_conv_w"] = nrm((CONV_K, CONV_CH), CONV_K ** -0.5)
    inp["l1_conv_b"] = nrm((CONV_CH,), 0.01)
    inp["l1_w_out"] = nrm((CONV_CH, d), CONV_CH ** -0.5)
    return inp


def reference(x, c, ctx, c_ctx,
              l0_norm_w, l0_w_mod, l0_b_mod, l0_w_in, l0_q_norm_w, l0_k_norm_w,
              l0_fwd_lam_re, l0_fwd_lam_im, l0_fwd_log_dt, l0_fwd_b_re, l0_fwd_b_im, l0_fwd_c_re, l0_fwd_c_im,
              l0_bwd_lam_re, l0_bwd_lam_im, l0_bwd_log_dt, l0_bwd_b_re, l0_bwd_b_im, l0_bwd_c_re, l0_bwd_c_im,
              l0_ssm_d, l0_w_glu, l0_b_glu, l0_w_out,
              l1_norm_w, l1_w_mod, l1_b_mod, l1_w_in, l1_conv_w, l1_conv_b, l1_w_out):
    cos, sin = axial_rope_tables(x.shape[1])
    layers = (
        dict(norm_w=l0_norm_w, w_mod=l0_w_mod, b_mod=l0_b_mod, w_in=l0_w_in,
             q_norm_w=l0_q_norm_w, k_norm_w=l0_k_norm_w,
             fwd=(l0_fwd_lam_re, l0_fwd_lam_im, l0_fwd_log_dt, l0_fwd_b_re, l0_fwd_b_im, l0_fwd_c_re, l0_fwd_c_im),
             bwd=(l0_bwd_lam_re, l0_bwd_lam_im, l0_bwd_log_dt, l0_bwd_b_re, l0_bwd_b_im, l0_bwd_c_re, l0_bwd_c_im),
             ssm_d=l0_ssm_d, w_glu=l0_w_glu, b_glu=l0_b_glu, w_out=l0_w_out),
        dict(norm_w=l1_norm_w, w_mod=l1_w_mod, b_mod=l1_b_mod, w_in=l1_w_in,
             conv_w=l1_conv_w, conv_b=l1_conv_b, w_out=l1_w_out),
    )
    h, hc = x, ctx
    for i in range(DEPTH):
        ctx_out = any(j % 2 == 0 for j in range(i + 1, DEPTH))
        if i % 2 == 0:
            h, hc = attn_ssm_layer(h, hc, layers[i], c, c_ctx, cos, sin, ctx_out)
        else:
            h, hc = shortconv_layer(h, hc, layers[i], c, c_ctx, ctx_out)
    return h
```

```python
import functools
import math

import jax
import jax.numpy as jnp
from jax import lax
from jax.experimental import pallas as pl
from jax.experimental.pallas import tpu as pltpu

F32 = jnp.float32
BF16 = jnp.bfloat16

EPS = 1e-6
HEAD_DIM = 128
Q_PER_KV = 4
GRID_W = 64
ROPE_THETA = 10000.0
CONV_K = 3

LANES = 128
SUBLANES = 8
S5_CHUNK = 16
S5_POW_ROWS = 32
VMEM_LIMIT_BYTES = 56 * 1024 * 1024


def _params(*semantics):
    return pltpu.CompilerParams(dimension_semantics=semantics, vmem_limit_bytes=VMEM_LIMIT_BYTES)


def _tile(dim, pref, mult):
    if dim <= pref:
        return dim
    t = (pref // mult) * mult
    while t > mult and dim % t:
        t -= mult
    assert dim % t == 0, (dim, pref, mult)
    return t


def _sigmoid(x):
    return 1.0 / (1.0 + jnp.exp(-x))


def _silu(x):
    return x * _sigmoid(x)


def _ada_kernel(c_ref, w_ref, b_ref, o_ref):
    s = _silu(c_ref[...]).astype(BF16)
    o_ref[...] = jnp.dot(s, w_ref[...].astype(BF16), preferred_element_type=F32) + b_ref[...]


def _ada_mod(cond, w_mod, b_mod):
    rows, d = cond.shape
    n = w_mod.shape[1]
    tn = _tile(n, 512, LANES)
    return pl.pallas_call(
        _ada_kernel,
        grid=(n // tn,),
        in_specs=[pl.BlockSpec((rows, d), lambda j: (0, 0)),
                  pl.BlockSpec((d, tn), lambda j: (0, j)),
                  pl.BlockSpec((1, tn), lambda j: (0, j))],
        out_specs=pl.BlockSpec((rows, tn), lambda j: (0, j)),
        out_shape=jax.ShapeDtypeStruct((rows, n), F32),
        compiler_params=_params("parallel"),
        name="ada_mod",
    )(cond, w_mod, b_mod.reshape(1, n))


def _modulated(x, nw, shift, scale):
    ms = jnp.mean(x * x, axis=-1, keepdims=True)
    return (x * lax.rsqrt(ms + EPS) * nw) * (1.0 + scale) + shift


def _norm_rope_heads(acc, nw, cos, sin, first_half, out_scale, o_ref):
    for h in range(acc.shape[1] // HEAD_DIM):
        a = acc[:, h * HEAD_DIM:(h + 1) * HEAD_DIM]
        ms = jnp.mean(a * a, axis=-1, keepdims=True)
        a = a * lax.rsqrt(ms + EPS) * nw
        swapped = jnp.where(first_half, pltpu.roll(a, HEAD_DIM - 32, 1), pltpu.roll(a, 32, 1))
        o_ref[:, h * HEAD_DIM:(h + 1) * HEAD_DIM] = ((a * cos + swapped * sin) * out_scale).astype(o_ref.dtype)


def _inproj0_kernel(bounds, x_ref, nw_ref, sh_ref, sc_ref, w_ref, cos_ref, sin_ref, qn_ref, kn_ref,
                    q_ref, k_ref, v_ref, ga_ref, u_ref, gs_ref, xn_scr):
    j = pl.program_id(1)
    e_q, e_k, e_v, e_ga, e_u = bounds

    @pl.when(j == 0)
    def _():
        xn_scr[...] = _modulated(x_ref[...], nw_ref[...], sh_ref[...], sc_ref[...]).astype(BF16)

    acc = jnp.dot(xn_scr[...], w_ref[...], preferred_element_type=F32)
    lane = lax.broadcasted_iota(jnp.int32, (acc.shape[0], HEAD_DIM), 1)
    first_half = (lane % 64) < 32

    @pl.when(j < e_q)
    def _():
        _norm_rope_heads(acc, qn_ref[...], cos_ref[...], sin_ref[...], first_half, HEAD_DIM ** -0.5, q_ref)

    @pl.when((j >= e_q) & (j < e_k))
    def _():
        _norm_rope_heads(acc, kn_ref[...], cos_ref[...], sin_ref[...], first_half, 1.0, k_ref)

    @pl.when((j >= e_k) & (j < e_v))
    def _():
        v_ref[...] = acc.astype(BF16)

    @pl.when((j >= e_v) & (j < e_ga))
    def _():
        ga_ref[...] = _silu(acc).astype(BF16)

    @pl.when((j >= e_ga) & (j < e_u))
    def _():
        u_ref[...] = acc

    @pl.when(j >= e_u)
    def _():
        gs_ref[...] = _silu(acc).astype(BF16)


def _inproj0(x2, norm_w, shift, scale, w_bf, cos_t, sin_t, qn, kn, widths, rows_per_batch, tm):
    m, d = x2.shape
    n = w_bf.shape[1]
    tn = _tile(min(widths), 512, LANES)
    assert all(w % tn == 0 for w in widths) and sum(widths) == n
    ends, off = [], 0
    for w in widths:
        off += w
        ends.append(off // tn)
    starts = [0] + ends[:-1]
    tpb = rows_per_batch // tm
    assert rows_per_batch % tm == 0 and m % tm == 0

    def out_spec(p):
        lo, hi = starts[p], ends[p]
        return pl.BlockSpec((tm, tn), lambda i, j: (i, jnp.clip(j - lo, 0, hi - lo - 1)))

    dts = (BF16, BF16, BF16, BF16, F32, BF16)
    return pl.pallas_call(
        functools.partial(_inproj0_kernel, tuple(ends[:5])),
        grid=(m // tm, n // tn),
        in_specs=[pl.BlockSpec((tm, d), lambda i, j: (i, 0)),
                  pl.BlockSpec((1, d), lambda i, j: (0, 0)),
                  pl.BlockSpec((None, 1, d), lambda i, j: (i // tpb, 0, 0)),
                  pl.BlockSpec((None, 1, d), lambda i, j: (i // tpb, 0, 0)),
                  pl.BlockSpec((d, tn), lambda i, j: (0, j)),
                  pl.BlockSpec((tm, HEAD_DIM), lambda i, j: (i % tpb, 0)),
                  pl.BlockSpec((tm, HEAD_DIM), lambda i, j: (i % tpb, 0)),
                  pl.BlockSpec((1, HEAD_DIM), lambda i, j: (0, 0)),
                  pl.BlockSpec((1, HEAD_DIM), lambda i, j: (0, 0))],
        out_specs=[out_spec(p) for p in range(6)],
        out_shape=[jax.ShapeDtypeStruct((m, w), dt) for w, dt in zip(widths, dts)],
        scratch_shapes=[pltpu.VMEM((tm, d), BF16)],
        compiler_params=_params("parallel", "arbitrary"),
        name="inproj0",
    )(x2, norm_w.reshape(1, d), shift, scale, w_bf, cos_t, sin_t, qn.reshape(1, HEAD_DIM), kn.reshape(1, HEAD_DIM))


def _attn_kernel(q_ref, k_ref, v_ref, g_ref, o_ref, m_scr, l_scr, acc_scr):
    kv = pl.program_id(3)

    @pl.when(kv == 0)
    def _():
        m_scr[...] = jnp.full(m_scr.shape, -jnp.inf, F32)
        l_scr[...] = jnp.zeros(l_scr.shape, F32)
        acc_scr[...] = jnp.zeros(acc_scr.shape, F32)

    k = k_ref[...]
    v = v_ref[...]
    for h in range(Q_PER_KV):
        qh = q_ref[:, h * HEAD_DIM:(h + 1) * HEAD_DIM]
        s = lax.dot_general(qh, k, (((1,), (1,)), ((), ())), preferred_element_type=F32)
        m_prev = m_scr[h]
        m_new = jnp.maximum(m_prev, jnp.max(s, axis=1, keepdims=True))
        alpha = jnp.exp(m_prev - m_new)
        p = jnp.exp(s - m_new[:, :1])
        l_scr[h] = alpha * l_scr[h] + jnp.sum(p, axis=1, keepdims=True)
        acc_scr[h] = alpha * acc_scr[h] + jnp.dot(p.astype(BF16), v, preferred_element_type=F32)
        m_scr[h] = m_new

    @pl.when(kv == pl.num_programs(3) - 1)
    def _():
        for h in range(Q_PER_KV):
            sl = slice(h * HEAD_DIM, (h + 1) * HEAD_DIM)
            o = acc_scr[h] / l_scr[h]
            o_ref[:, sl] = (o * g_ref[:, sl].astype(F32)).astype(o_ref.dtype)


def _attention(q, k_all, v_all, gate, tq, tk):
    b, l, aw = q.shape
    s_len, kvw = k_all.shape[1], k_all.shape[2]
    kvh = kvw // HEAD_DIM
    gw = Q_PER_KV * HEAD_DIM
    assert aw == kvh * gw
    return pl.pallas_call(
        _attn_kernel,
        grid=(b, kvh, l // tq, s_len // tk),
        in_specs=[pl.BlockSpec((None, tq, gw), lambda bi, h, qi, ki: (bi, qi, h)),
                  pl.BlockSpec((None, tk, HEAD_DIM), lambda bi, h, qi, ki: (bi, ki, h)),
                  pl.BlockSpec((None, tk, HEAD_DIM), lambda bi, h, qi, ki: (bi, ki, h)),
                  pl.BlockSpec((None, tq, gw), lambda bi, h, qi, ki: (bi, qi, h))],
        out_specs=pl.BlockSpec((None, tq, gw), lambda bi, h, qi, ki: (bi, qi, h)),
        out_shape=jax.ShapeDtypeStruct((b, l, aw), BF16),
        scratch_shapes=[pltpu.VMEM((Q_PER_KV, tq, HEAD_DIM), F32),
                        pltpu.VMEM((Q_PER_KV, tq, HEAD_DIM), F32),
                        pltpu.VMEM((Q_PER_KV, tq, HEAD_DIM), F32)],
        compiler_params=_params("parallel", "parallel", "parallel", "arbitrary"),
        name="attention",
    )(q, k_all, v_all, gate)


def _s5_param_kernel(reverse, lr_ref, li_ref, ldt_ref, b1_ref, bsw_ref, c1_ref, csw_ref,
                     w_ref, vt_ref, kt_ref, a_ref):
    t = S5_CHUNK
    lr, li = lr_ref[...], li_ref[...]
    dt = jnp.exp(ldt_ref[...])
    lane = lax.broadcasted_iota(jnp.int32, (1, LANES), 1)
    neg_first = jnp.where(lane < 64, -1.0, 1.0).astype(F32)
    jj = lax.broadcasted_iota(jnp.int32, (S5_POW_ROWS, LANES), 0).astype(F32)
    mag = jnp.exp(lr * dt * jj)
    ang = li * dt * jj
    p_re = mag * jnp.cos(ang)
    p_im = mag * jnp.sin(ang)

    a_re, a_im = p_re[1:2], p_im[1:2]
    den = lr * lr + li * li
    n_re = a_re - 1.0
    coef_re = (n_re * lr + a_im * li) / den
    coef_im = (a_im * lr - n_re * li) / den
    b1 = b1_ref[...]
    b2 = bsw_ref[...] * neg_first
    bb1 = coef_re * b1 + coef_im * b2
    bb2 = coef_re * b2 - coef_im * b1

    c1 = c1_ref[...] * (-neg_first)
    csw = csw_ref[...]

    xs = []
    for j in range(t):
        x_j = p_re[j:j + 1] * bb1 + p_im[j:j + 1] * bb2
        xs.append(x_j)
        blk = (t - 1 - j) if not reverse else j
        w_ref[blk * 16:(blk + 1) * 16, :] = x_j
        m = (j + 1) if not reverse else (t - j)
        vt_ref[j * 16:(j + 1) * 16, :] = p_re[m:m + 1] * c1 - p_im[m:m + 1] * csw
    x_all = jnp.concatenate(xs, axis=0)
    kt_ref[...] = lax.dot_general(x_all, c1, (((1,), (1,)), ((), ())),
                                  precision=lax.Precision.HIGHEST, preferred_element_type=F32)
    a_ref[...] = jnp.zeros(a_ref.shape, F32)
    a_ref[0:1, :] = p_re[t:t + 1]
    a_ref[1:2, :] = p_im[t:t + 1] * neg_first


def _s5_params(prm, reverse):
    lam_re, lam_im, log_dt, b_re, b_im, c_re, c_im = prm
    g, p = lam_re.shape
    i = b_re.shape[2]
    assert 2 * p == LANES and i * S5_CHUNK == 256

    def dup(a):
        return jnp.concatenate([a, a], axis=-1).reshape(g, 1, LANES).astype(F32)

    bt_re, bt_im = jnp.swapaxes(b_re, 1, 2).astype(F32), jnp.swapaxes(b_im, 1, 2).astype(F32)
    b1 = jnp.concatenate([bt_re, bt_im], axis=-1)
    bsw = jnp.concatenate([bt_im, bt_re], axis=-1)
    c1 = jnp.concatenate([c_re, c_im], axis=-1).astype(F32)
    csw = jnp.concatenate([c_im, c_re], axis=-1).astype(F32)
    ldt = jnp.broadcast_to(log_dt.astype(F32)[:, None, None], (g, 1, LANES))
    row = pl.BlockSpec((None, 1, LANES), lambda gi: (gi, 0, 0))
    mat = pl.BlockSpec((None, i, LANES), lambda gi: (gi, 0, 0))
    big = pl.BlockSpec((None, 256, LANES), lambda gi: (gi, 0, 0))
    return pl.pallas_call(
        functools.partial(_s5_param_kernel, reverse),
        grid=(g,),
        in_specs=[row, row, row, mat, mat, mat, mat],
        out_specs=[big, big,
                   pl.BlockSpec((None, 256, i), lambda gi: (gi, 0, 0)),
                   pl.BlockSpec((None, SUBLANES, LANES), lambda gi: (gi, 0, 0))],
        out_shape=[jax.ShapeDtypeStruct((g, 256, LANES), F32),
                   jax.ShapeDtypeStruct((g, 256, LANES), F32),
                   jax.ShapeDtypeStruct((g, 256, i), F32),
                   jax.ShapeDtypeStruct((g, SUBLANES, LANES), F32)],
        compiler_params=_params("parallel"),
        name="s5_params",
    )(dup(lam_re), dup(lam_im), ldt, b1, bsw, c1, csw)


def _toeplitz(kt_f, kt_b):
    g = kt_f.shape[0]
    t = S5_CHUNK
    i = kt_f.shape[2]
    kf = kt_f.reshape(g, t, i, i)
    kb = kt_b.reshape(g, t, i, i)
    s_idx = jnp.arange(t)[:, None]
    t_idx = jnp.arange(t)[None, :]
    lag = t_idx - s_idx
    f_part = kf[:, jnp.clip(lag, 0, t - 1)]
    b_part = kb[:, jnp.clip(-lag, 0, t - 1)]
    full = jnp.where((lag >= 0)[None, :, :, None, None], f_part, b_part)
    return jnp.transpose(full, (0, 1, 3, 2, 4)).reshape(g, t * i, t * i)


def _slot_transpose8(vs, slot):
    vs = list(vs)
    for d in (4, 2, 1):
        keep = (slot & d) == 0
        new = list(vs)
        for r in range(8):
            if r & d == 0:
                a, b = vs[r], vs[r + d]
                new[r] = jnp.where(keep, a, pltpu.roll(b, 16 * d, 1))
                new[r + d] = jnp.where(keep, pltpu.roll(a, LANES - 16 * d, 1), b)
        vs = new
    return vs


def _gather_chunks(src_ref, dst_scr, n, rb):
    t = S5_CHUNK
    for r0 in range(0, n, rb):
        slot = lax.broadcasted_iota(jnp.int32, (rb, LANES), 1) // 16
        for hi in range(2):
            pieces = [src_ref[pl.ds(r0 * t + hi * 8 + r, rb, stride=t), :] for r in range(8)]
            outs = _slot_transpose8(pieces, slot)
            for gi in range(8):
                dst_scr[gi, r0:r0 + rb, hi * LANES:(hi + 1) * LANES] = outs[gi].astype(BF16)


def _s5_kernel(n, nc, rb, ux_ref, uc_ref, toep_ref, wf_ref, wb_ref, vf_ref, vb_ref, af_ref, ab_ref,
               y_ref, u_scr, ucx_scr, wf_scr, wfs_scr, wb_scr, wbs_scr, hf_scr, gb_scr):
    t = S5_CHUNK
    ntot = n + nc
    _gather_chunks(ux_ref, u_scr, n, rb)
    _gather_chunks(uc_ref, ucx_scr, nc, min(rb, nc))

    for gi in range(8):
        ls = slice(gi * LANES, (gi + 1) * LANES)
        ug, ucg = u_scr[gi], ucx_scr[gi]
        wf, wb = wf_ref[gi], wb_ref[gi]
        f_c = jnp.dot(ucg, wf, preferred_element_type=F32)
        f_x = jnp.dot(ug, wf, preferred_element_type=F32)
        b_x = jnp.dot(ug, wb, preferred_element_type=F32)
        b_c = jnp.dot(ucg, wb, preferred_element_type=F32)
        wf_scr[0:nc, ls] = f_c
        wf_scr[nc:ntot, ls] = f_x
        wb_scr[0:n, ls] = b_x
        wb_scr[n:ntot, ls] = b_c
        wfs_scr[0:nc, ls] = pltpu.roll(f_c, 64, 1)
        wfs_scr[nc:ntot, ls] = pltpu.roll(f_x, 64, 1)
        wbs_scr[0:n, ls] = pltpu.roll(b_x, 64, 1)
        wbs_scr[n:ntot, ls] = pltpu.roll(b_c, 64, 1)

    a1f = jnp.concatenate([af_ref[gi, 0:1, :] for gi in range(8)], axis=1)
    a2f = jnp.concatenate([af_ref[gi, 1:2, :] for gi in range(8)], axis=1)
    a1b = jnp.concatenate([ab_ref[gi, 0:1, :] for gi in range(8)], axis=1)
    a2b = jnp.concatenate([ab_ref[gi, 1:2, :] for gi in range(8)], axis=1)

    def step(c, carry):
        h, hs, g, gs = carry
        hf_scr[pl.ds(c, 1), :] = h
        r = ntot - 1 - c
        gb_scr[pl.ds(r, 1), :] = g
        h_new = a1f * h + a2f * hs + wf_scr[pl.ds(c, 1), :]
        hs_new = a1f * hs - a2f * h + wfs_scr[pl.ds(c, 1), :]
        g_new = a1b * g + a2b * gs + wb_scr[pl.ds(r, 1), :]
        gs_new = a1b * gs - a2b * g + wbs_scr[pl.ds(r, 1), :]
        return h_new, hs_new, g_new, gs_new

    zero = jnp.zeros((1, 8 * LANES), F32)
    lax.fori_loop(0, ntot, step, (zero, zero, zero, zero))

    nt = (((1,), (1,)), ((), ()))
    for r0 in range(0, n, rb):
        slot = lax.broadcasted_iota(jnp.int32, (rb, LANES), 1) // 16
        ys = []
        for gi in range(8):
            ls = slice(gi * LANES, (gi + 1) * LANES)
            y = jnp.dot(u_scr[gi, r0:r0 + rb, :], toep_ref[gi], preferred_element_type=F32)
            y += lax.dot_general(hf_scr[nc + r0:nc + r0 + rb, ls].astype(BF16), vf_ref[gi], nt,
                                 preferred_element_type=F32)
            y += lax.dot_general(gb_scr[r0:r0 + rb, ls].astype(BF16), vb_ref[gi], nt,
                                 preferred_element_type=F32)
            ys.append(y)
        for hi in range(2):
            outs = _slot_transpose8([ys[gi][:, hi * LANES:(hi + 1) * LANES] for gi in range(8)], slot)
            for r in range(8):
                y_ref[pl.ds(r0 * t + hi * 8 + r, rb, stride=t), :] = outs[r]


def _s5(u, uc, toep, wf, wb, vf, vb, af, ab):
    b, l, c = u.shape
    lc = uc.shape[1]
    t = S5_CHUNK
    n, nc = l // t, lc // t
    assert l % t == 0 and lc % t == 0 and c % LANES == 0 and n % SUBLANES == 0 and nc % SUBLANES == 0
    rb = _tile(n, 128, SUBLANES)
    ntot = n + nc
    gmat = lambda last: pl.BlockSpec((8, 256, last), lambda bi, ci: (ci, 0, 0))
    return pl.pallas_call(
        functools.partial(_s5_kernel, n, nc, rb),
        grid=(b, c // LANES),
        in_specs=[pl.BlockSpec((None, l, LANES), lambda bi, ci: (bi, 0, ci)),
                  pl.BlockSpec((None, lc, LANES), lambda bi, ci: (bi, 0, ci)),
                  gmat(256), gmat(LANES), gmat(LANES), gmat(LANES), gmat(LANES),
                  pl.BlockSpec((8, SUBLANES, LANES), lambda bi, ci: (ci, 0, 0)),
                  pl.BlockSpec((8, SUBLANES, LANES), lambda bi, ci: (ci, 0, 0))],
        out_specs=pl.BlockSpec((None, l, LANES), lambda bi, ci: (bi, 0, ci)),
        out_shape=jax.ShapeDtypeStruct((b, l, c), F32),
        scratch_shapes=[pltpu.VMEM((8, n, 256), BF16),
                        pltpu.VMEM((8, nc, 256), BF16),
                        pltpu.VMEM((ntot, 8 * LANES), F32),
                        pltpu.VMEM((ntot, 8 * LANES), F32),
                        pltpu.VMEM((ntot, 8 * LANES), F32),
                        pltpu.VMEM((ntot, 8 * LANES), F32),
                        pltpu.VMEM((ntot, 8 * LANES), F32),
                        pltpu.VMEM((ntot, 8 * LANES), F32)],
        compiler_params=_params("parallel", "parallel"),
        name="s5_scan",
    )(u, uc, toep, wf, wb, vf, vb, af, ab)


def _glu_kernel(nj, tn, y_ref, u_ref, d_ref, w_ref, b_ref, g_ref, o_ref, v32_scr, vb_scr):
    j = pl.program_id(1)

    @pl.when(j == 0)
    def _():
        v = jax.nn.gelu(y_ref[...] + d_ref[...] * u_ref[...])
        vb_scr[...] = v.astype(BF16)
        for jj in range(nj):
            v32_scr[jj] = v[:, jj * tn:(jj + 1) * tn]

    acc = jnp.dot(vb_scr[...], w_ref[...], preferred_element_type=F32) + b_ref[...]
    o_ref[...] = (v32_scr[j] * _sigmoid(acc) * g_ref[...].astype(F32)).astype(o_ref.dtype)


def _glu(y, u, d_skip, w_bf, b_glu, gate, tm):
    m, c = y.shape
    tn = _tile(c, 512, LANES)
    nj = c // tn
    return pl.pallas_call(
        functools.partial(_glu_kernel, nj, tn),
        grid=(m // tm, nj),
        in_specs=[pl.BlockSpec((tm, c), lambda i, j: (i, 0)),
                  pl.BlockSpec((tm, c), lambda i, j: (i, 0)),
                  pl.BlockSpec((1, c), lambda i, j: (0, 0)),
                  pl.BlockSpec((c, tn), lambda i, j: (0, j)),
                  pl.BlockSpec((1, tn), lambda i, j: (0, j)),
                  pl.BlockSpec((tm, tn), lambda i, j: (i, j))],
        out_specs=pl.BlockSpec((tm, tn), lambda i, j: (i, j)),
        out_shape=jax.ShapeDtypeStruct((m, c), BF16),
        scratch_shapes=[pltpu.VMEM((nj, tm, tn), F32), pltpu.VMEM((tm, c), BF16)],
        compiler_params=_params("parallel", "arbitrary"),
        name="ssm_glu",
    )(y, u, d_skip.reshape(1, c), w_bf, b_glu.reshape(1, c), gate)


def _outproj0_kernel(a_ref, s_ref, wa_ref, ws_ref, x_ref, g_ref, o_ref):
    acc = jnp.dot(a_ref[...], wa_ref[...], preferred_element_type=F32)
    acc += jnp.dot(s_ref[...], ws_ref[...], preferred_element_type=F32)
    o_ref[...] = x_ref[...] + g_ref[...] * acc


def _outproj0(att, ssm, w_bf, x2, gate, rows_per_batch, tm):
    m, ka = att.shape
    ks = ssm.shape[1]
    d = w_bf.shape[1]
    tn = _tile(d, 512, LANES)
    tpb = rows_per_batch // tm
    assert ka % SUBLANES == 0 and rows_per_batch % tm == 0
    ka_blocks = ka // ks
    assert ka == ks * ka_blocks
    return pl.pallas_call(
        _outproj0_kernel,
        grid=(m // tm, d // tn),
        in_specs=[pl.BlockSpec((tm, ka), lambda i, j: (i, 0)),
                  pl.BlockSpec((tm, ks), lambda i, j: (i, 0)),
                  pl.BlockSpec((ka, tn), lambda i, j: (0, j)),
                  pl.BlockSpec((ks, tn), lambda i, j: (ka_blocks, j)),
                  pl.BlockSpec((tm, tn), lambda i, j: (i, j)),
                  pl.BlockSpec((None, 1, tn), lambda i, j: (i // tpb, 0, j))],
        out_specs=pl.BlockSpec((tm, tn), lambda i, j: (i, j)),
        out_shape=jax.ShapeDtypeStruct((m, d), F32),
        compiler_params=_params("parallel", "parallel"),
        name="outproj0",
    )(att, ssm, w_bf, w_bf, x2, gate)


def _inproj1_kernel(x_ref, nw_ref, sh_ref, sc_ref, wb_ref, wc_ref, wx_ref, wg_ref, y_ref, bg_ref, xn_scr):
    @pl.when(pl.program_id(1) == 0)
    def _():
        xn_scr[...] = _modulated(x_ref[...], nw_ref[...], sh_ref[...], sc_ref[...]).astype(BF16)

    xn = xn_scr[...]
    b_gate = jnp.dot(xn, wb_ref[...], preferred_element_type=F32)
    c_gate = jnp.dot(xn, wc_ref[...], preferred_element_type=F32)
    xin = jnp.dot(xn, wx_ref[...], preferred_element_type=F32)
    g = jnp.dot(xn, wg_ref[...], preferred_element_type=F32)
    y_ref[...] = c_gate * xin
    bg_ref[...] = (b_gate * _silu(g)).astype(BF16)


def _inproj1(x2, norm_w, shift, scale, w_bf, rows_per_batch, tm):
    m, d = x2.shape
    c = w_bf.shape[1] // 4
    tn = _tile(c, 256, LANES)
    nj = c // tn
    tpb = rows_per_batch // tm
    assert rows_per_batch % tm == 0

    def wspec(p):
        return pl.BlockSpec((d, tn), lambda i, j: (0, p * nj + j))

    return pl.pallas_call(
        _inproj1_kernel,
        grid=(m // tm, nj),
        in_specs=[pl.BlockSpec((tm, d), lambda i, j: (i, 0)),
                  pl.BlockSpec((1, d), lambda i, j: (0, 0)),
                  pl.BlockSpec((None, 1, d), lambda i, j: (i // tpb, 0, 0)),
                  pl.BlockSpec((None, 1, d), lambda i, j: (i // tpb, 0, 0)),
                  wspec(0), wspec(1), wspec(2), wspec(3)],
        out_specs=[pl.BlockSpec((tm, tn), lambda i, j: (i, j)),
                   pl.BlockSpec((tm, tn), lambda i, j: (i, j))],
        out_shape=[jax.ShapeDtypeStruct((m, c), F32), jax.ShapeDtypeStruct((m, c), BF16)],
        scratch_shapes=[pltpu.VMEM((tm, d), BF16)],
        compiler_params=_params("parallel", "arbitrary"),
        name="inproj1",
    )(x2, norm_w.reshape(1, d), shift, scale, w_bf, w_bf, w_bf, w_bf)


def _outproj1_kernel(tpb, y_ref, yp_ref, yn_ref, bg_ref, cw_ref, cb_ref, w_ref, x_ref, g_ref, o_ref, lhs_scr):
    i = pl.program_id(0)

    @pl.when(pl.program_id(1) == 0)
    def _():
        y = y_ref[...]
        tm = y.shape[0]
        row = lax.broadcasted_iota(jnp.int32, y.shape, 0)
        first = (i % tpb) == 0
        last = (i % tpb) == tpb - 1
        halo_prev = jnp.where(first, 0.0, yp_ref[SUBLANES - 1:SUBLANES, :])
        halo_next = jnp.where(last, 0.0, yn_ref[0:1, :])
        y_prev = jnp.where(row == 0, halo_prev, pltpu.roll(y, 1, 0))
        y_next = jnp.where(row == tm - 1, halo_next, pltpu.roll(y, tm - 1, 0))
        conv = cb_ref[...] + y_prev * cw_ref[0:1, :] + y * cw_ref[1:2, :] + y_next * cw_ref[2:3, :]
        lhs_scr[...] = (conv * bg_ref[...].astype(F32)).astype(BF16)

    acc = jnp.dot(lhs_scr[...], w_ref[...], preferred_element_type=F32)
    o_ref[...] = x_ref[...] + g_ref[...] * acc


def _outproj1(y, bg, conv_w, conv_b, w_bf, x2, gate, rows_per_batch, tm):
    m, c = y.shape
    d = w_bf.shape[1]
    tn = _tile(d, 512, LANES)
    tpb = rows_per_batch // tm
    hb = tm // SUBLANES
    nhb = m // SUBLANES
    assert rows_per_batch % tm == 0 and tm % SUBLANES == 0
    return pl.pallas_call(
        functools.partial(_outproj1_kernel, tpb),
        grid=(m // tm, d // tn),
        in_specs=[pl.BlockSpec((tm, c), lambda i, j: (i, 0)),
                  pl.BlockSpec((SUBLANES, c), lambda i, j: (jnp.maximum(i * hb - 1, 0), 0)),
                  pl.BlockSpec((SUBLANES, c), lambda i, j: (jnp.minimum((i + 1) * hb, nhb - 1), 0)),
                  pl.BlockSpec((tm, c), lambda i, j: (i, 0)),
                  pl.BlockSpec((CONV_K, c), lambda i, j: (0, 0)),
                  pl.BlockSpec((1, c), lambda i, j: (0, 0)),
                  pl.BlockSpec((c, tn), lambda i, j: (0, j)),
                  pl.BlockSpec((tm, tn), lambda i, j: (i, j)),
                  pl.BlockSpec((None, 1, tn), lambda i, j: (i // tpb, 0, j))],
        out_specs=pl.BlockSpec((tm, tn), lambda i, j: (i, j)),
        out_shape=jax.ShapeDtypeStruct((m, d), F32),
        scratch_shapes=[pltpu.VMEM((tm, c), BF16)],
        compiler_params=_params("parallel", "arbitrary"),
        name="outproj1",
    )(y, y, y, bg, conv_w, conv_b.reshape(1, c), w_bf, x2, gate)


def _rope_tables(n_tok):
    rows = n_tok // GRID_W
    r, col = jnp.meshgrid(jnp.arange(rows, dtype=F32), jnp.arange(GRID_W, dtype=F32), indexing="ij")
    axis_dim = HEAD_DIM // 2
    inv_freq = ROPE_THETA ** (-jnp.arange(0, axis_dim, 2, dtype=F32) / axis_dim)
    ar = r.reshape(-1)[:, None] * inv_freq
    ac = col.reshape(-1)[:, None] * inv_freq
    cos_t = jnp.concatenate([jnp.cos(ar), jnp.cos(ar), jnp.cos(ac), jnp.cos(ac)], axis=1)
    sin_t = jnp.concatenate([-jnp.sin(ar), jnp.sin(ar), -jnp.sin(ac), jnp.sin(ac)], axis=1)
    return cos_t, sin_t


def kernel(x, c, ctx, c_ctx, l0_norm_w, l0_w_mod, l0_b_mod, l0_w_in, l0_q_norm_w, l0_k_norm_w, l0_fwd_lam_re, l0_fwd_lam_im, l0_fwd_log_dt, l0_fwd_b_re, l0_fwd_b_im, l0_fwd_c_re, l0_fwd_c_im, l0_bwd_lam_re, l0_bwd_lam_im, l0_bwd_log_dt, l0_bwd_b_re, l0_bwd_b_im, l0_bwd_c_re, l0_bwd_c_im, l0_ssm_d, l0_w_glu, l0_b_glu, l0_w_out, l1_norm_w, l1_w_mod, l1_b_mod, l1_w_in, l1_conv_w, l1_conv_b, l1_w_out):
    b, l, d = x.shape
    lc = ctx.shape[1]
    ssm_w = l0_ssm_d.shape[0]
    att_w = l0_w_out.shape[0] - ssm_w
    kv_w = att_w // Q_PER_KV
    widths = (att_w, kv_w, kv_w, att_w, ssm_w, ssm_w)
    assert sum(widths) == l0_w_in.shape[1]

    pad = (-(b + 1)) % 16
    cond = jnp.concatenate([c, c_ctx[None, :], jnp.zeros((pad, d), F32)], axis=0)
    mod0 = _ada_mod(cond, l0_w_mod, l0_b_mod)
    mod1 = _ada_mod(cond, l1_w_mod, l1_b_mod)
    shift0, scale0, gate0 = (mod0[:b, k * d:(k + 1) * d].reshape(b, 1, d) for k in range(3))
    shift0c, scale0c = (jnp.broadcast_to(mod0[b, k * d:(k + 1) * d].reshape(1, 1, d), (b, 1, d)) for k in range(2))
    shift1, scale1, gate1 = (mod1[:b, k * d:(k + 1) * d].reshape(b, 1, d) for k in range(3))

    x2 = x.reshape(b * l, d)
    ctx2 = ctx.reshape(b * lc, d)
    w_in0 = l0_w_in.astype(BF16)

    cos_t, sin_t = _rope_tables(l)
    tm = _tile(l, 512, SUBLANES)
    q, k, v, g_att, u, g_ssm = _inproj0(x2, l0_norm_w, shift0, scale0, w_in0, cos_t, sin_t,
                                        l0_q_norm_w, l0_k_norm_w, widths, l, tm)
    tmc = _tile(lc, 256, SUBLANES)
    ones_t, zeros_t = jnp.ones((lc, HEAD_DIM), F32), jnp.zeros((lc, HEAD_DIM), F32)
    _, kc, vc, _, uc, _ = _inproj0(ctx2, l0_norm_w, shift0c, scale0c, w_in0, ones_t, zeros_t,
                                   l0_q_norm_w, l0_k_norm_w, widths, lc, tmc)

    k_all = jnp.concatenate([kc.reshape(b, lc, kv_w), k.reshape(b, l, kv_w)], axis=1)
    v_all = jnp.concatenate([vc.reshape(b, lc, kv_w), v.reshape(b, l, kv_w)], axis=1)
    tq = _tile(l, 512, SUBLANES)
    tk = _tile(l + lc, 1408, LANES)
    att = _attention(q.reshape(b, l, att_w), k_all, v_all, g_att.reshape(b, l, att_w), tq, tk)

    fwd = (l0_fwd_lam_re, l0_fwd_lam_im, l0_fwd_log_dt, l0_fwd_b_re, l0_fwd_b_im, l0_fwd_c_re, l0_fwd_c_im)
    bwd = (l0_bwd_lam_re, l0_bwd_lam_im, l0_bwd_log_dt, l0_bwd_b_re, l0_bwd_b_im, l0_bwd_c_re, l0_bwd_c_im)
    wf, vf, ktf, af = _s5_params(fwd, False)
    wb, vb, ktb, ab = _s5_params(bwd, True)
    ktf = ktf.at[:, :ktb.shape[2], :].add(ktb[:, :ktb.shape[2], :])
    toep = _toeplitz(ktf, ktb).astype(BF16)
    y = _s5(u.reshape(b, l, ssm_w), uc.reshape(b, lc, ssm_w), toep,
            wf.astype(BF16), wb.astype(BF16), vf.astype(BF16), vb.astype(BF16), af, ab)
    ssm = _glu(y.reshape(b * l, ssm_w), u, l0_ssm_d, l0_w_glu.astype(BF16), l0_b_glu, g_ssm, tm)

    tmo = _tile(l, 1024, SUBLANES)
    h1 = _outproj0(att.reshape(b * l, att_w), ssm, l0_w_out.astype(BF16), x2, gate0, l, tmo)

    yv, bg = _inproj1(h1, l1_norm_w, shift1, scale1, l1_w_in.astype(BF16), l, tm)
    h2 = _outproj1(yv, bg, l1_conv_w, l1_conv_b, l1_w_out.astype(BF16), h1, gate1, l, tm)
    return h2.reshape(b, l, d)
```

```python
import functools
import math

import jax
import jax.numpy as jnp
from jax import lax
from jax.experimental import pallas as pl
from jax.experimental.pallas import tpu as pltpu

F32 = jnp.float32
BF16 = jnp.bfloat16

EPS = 1e-6
HEAD_DIM = 128
Q_PER_KV = 4
GRID_W = 64
ROPE_THETA = 10000.0
CONV_K = 3
Q_SCALE = HEAD_DIM ** -0.5 * math.log2(math.e)
SCORE_BOUND_NO_SHIFT = 40.0

LANES = 128
SUBLANES = 8
HALO = 16
S5_CHUNK = 16
S5_POW_ROWS = 32
VMEM_LIMIT_BYTES = 56 * 1024 * 1024


def _params(*semantics):
    return pltpu.CompilerParams(dimension_semantics=semantics, vmem_limit_bytes=VMEM_LIMIT_BYTES)


def _tile(dim, pref, mult):
    if dim <= pref:
        return dim
    t = (pref // mult) * mult
    while t > mult and dim % t:
        t -= mult
    assert dim % t == 0, (dim, pref, mult)
    return t


def _sigmoid(x):
    return 1.0 / (1.0 + jnp.exp(-x))


def _silu(x):
    return x * _sigmoid(x)


def _ada_kernel(c_ref, w_ref, b_ref, o_ref):
    s = _silu(c_ref[...]).astype(BF16)
    o_ref[...] = jnp.dot(s, w_ref[...].astype(BF16), preferred_element_type=F32) + b_ref[...]


def _ada_mod(cond, w_mod, b_mod):
    rows, d = cond.shape
    n = w_mod.shape[1]
    tn = _tile(n, 512, LANES)
    return pl.pallas_call(
        _ada_kernel,
        grid=(n // tn,),
        in_specs=[pl.BlockSpec((rows, d), lambda j: (0, 0)),
                  pl.BlockSpec((d, tn), lambda j: (0, j)),
                  pl.BlockSpec((1, tn), lambda j: (0, j))],
        out_specs=pl.BlockSpec((rows, tn), lambda j: (0, j)),
        out_shape=jax.ShapeDtypeStruct((rows, n), F32),
        compiler_params=_params("parallel"),
        name="ada_mod",
    )(cond, w_mod, b_mod.reshape(1, n))


def _modulated(x, nw, shift, scale):
    ms = jnp.mean(x * x, axis=-1, keepdims=True)
    return (x * lax.rsqrt(ms + EPS) * nw) * (1.0 + scale) + shift


def _norm_rope_heads(acc, nw, cos, sin, first_half, out_scale, o_ref, col0):
    for h in range(acc.shape[1] // HEAD_DIM):
        a = acc[:, h * HEAD_DIM:(h + 1) * HEAD_DIM]
        ms = jnp.mean(a * a, axis=-1, keepdims=True)
        a = a * lax.rsqrt(ms + EPS) * nw
        swapped = jnp.where(first_half, pltpu.roll(a, HEAD_DIM - 32, 1), pltpu.roll(a, 32, 1))
        out = ((a * cos + swapped * sin) * out_scale).astype(o_ref.dtype)
        o_ref[:, col0 + h * HEAD_DIM:col0 + (h + 1) * HEAD_DIM] = out


def _inproj0_kernel(bounds, x_ref, nw_ref, sh_ref, sc_ref, w_ref, cos_ref, sin_ref, qn_ref, kn_ref,
                    q_ref, k_ref, v_ref, ga_ref, u_ref, gs_ref, xn_scr):
    j = pl.program_id(1)
    e_q, e_k, e_v, e_ga, e_u = bounds
    tm, tn = q_ref.shape
    sub = min(tn, 2 * LANES)

    @pl.when(j == 0)
    def _():
        xn_scr[...] = _modulated(x_ref[...], nw_ref[...], sh_ref[...], sc_ref[...]).astype(BF16)

    def per_subtile(epilogue):
        for c0 in range(0, tn, sub):
            epilogue(c0, jnp.dot(xn_scr[...], w_ref[:, c0:c0 + sub], preferred_element_type=F32))

    def rope(norm_ref, scale, o_ref):
        lane = lax.broadcasted_iota(jnp.int32, (tm, HEAD_DIM), 1)
        first_half = (lane % 64) < 32
        per_subtile(lambda c0, acc: _norm_rope_heads(acc, norm_ref[...], cos_ref[...], sin_ref[...],
                                                     first_half, scale, o_ref, c0))

    def store(o_ref, fn):
        def epilogue(c0, acc):
            o_ref[:, c0:c0 + sub] = fn(acc).astype(o_ref.dtype)
        per_subtile(epilogue)

    @pl.when(j < e_q)
    def _():
        rope(qn_ref, Q_SCALE, q_ref)

    @pl.when((j >= e_q) & (j < e_k))
    def _():
        rope(kn_ref, 1.0, k_ref)

    @pl.when((j >= e_k) & (j < e_v))
    def _():
        store(v_ref, lambda acc: acc)

    @pl.when((j >= e_v) & (j < e_ga))
    def _():
        store(ga_ref, _silu)

    @pl.when((j >= e_ga) & (j < e_u))
    def _():
        store(u_ref, lambda acc: acc)

    @pl.when(j >= e_u)
    def _():
        store(gs_ref, _silu)


def _inproj0(x2, norm_w, shift, scale, w_bf, cos_t, sin_t, qn, kn, widths, rows_per_batch, tm):
    m, d = x2.shape
    n = w_bf.shape[1]
    tn = _tile(min(widths), 512, LANES)
    assert all(w % tn == 0 for w in widths) and sum(widths) == n
    ends, off = [], 0
    for w in widths:
        off += w
        ends.append(off // tn)
    starts = [0] + ends[:-1]
    tpb = rows_per_batch // tm
    assert rows_per_batch % tm == 0 and m % tm == 0

    def out_spec(p):
        lo, hi = starts[p], ends[p]
        return pl.BlockSpec((tm, tn), lambda i, j: (i, jnp.clip(j - lo, 0, hi - lo - 1)))

    dts = (BF16, BF16, BF16, BF16, F32, BF16)
    return pl.pallas_call(
        functools.partial(_inproj0_kernel, tuple(ends[:5])),
        grid=(m // tm, n // tn),
        in_specs=[pl.BlockSpec((tm, d), lambda i, j: (i, 0)),
                  pl.BlockSpec((1, d), lambda i, j: (0, 0)),
                  pl.BlockSpec((None, 1, d), lambda i, j: (i // tpb, 0, 0)),
                  pl.BlockSpec((None, 1, d), lambda i, j: (i // tpb, 0, 0)),
                  pl.BlockSpec((d, tn), lambda i, j: (0, j)),
                  pl.BlockSpec((tm, HEAD_DIM), lambda i, j: (i % tpb, 0)),
                  pl.BlockSpec((tm, HEAD_DIM), lambda i, j: (i % tpb, 0)),
                  pl.BlockSpec((1, HEAD_DIM), lambda i, j: (0, 0)),
                  pl.BlockSpec((1, HEAD_DIM), lambda i, j: (0, 0))],
        out_specs=[out_spec(p) for p in range(6)],
        out_shape=[jax.ShapeDtypeStruct((m, w), dt) for w, dt in zip(widths, dts)],
        scratch_shapes=[pltpu.VMEM((tm, d), BF16)],
        compiler_params=_params("parallel", "arbitrary"),
        name="inproj0",
    )(x2, norm_w.reshape(1, d), shift, scale, w_bf, cos_t, sin_t, qn.reshape(1, HEAD_DIM), kn.reshape(1, HEAD_DIM))


def _attn_kernel(q_ref, k_ref, v_ref, g_ref, o_ref, m_scr, l_scr, acc_scr):
    kv = pl.program_id(3)

    @pl.when(kv == 0)
    def _():
        m_scr[...] = jnp.full(m_scr.shape, -jnp.inf, F32)
        l_scr[...] = jnp.zeros(l_scr.shape, F32)
        acc_scr[...] = jnp.zeros(acc_scr.shape, F32)

    k = k_ref[...]
    v = v_ref[...]
    for h in range(Q_PER_KV):
        qh = q_ref[:, h * HEAD_DIM:(h + 1) * HEAD_DIM]
        s = lax.dot_general(qh, k, (((1,), (1,)), ((), ())), preferred_element_type=F32)
        m_prev = m_scr[h]
        m_new = jnp.maximum(m_prev, jnp.max(s, axis=1, keepdims=True))
        alpha = jnp.exp2(m_prev - m_new)
        p = jnp.exp2(s - m_new[:, :1])
        l_scr[h] = alpha * l_scr[h] + jnp.sum(p, axis=1, keepdims=True)
        acc_scr[h] = alpha * acc_scr[h] + jnp.dot(p.astype(BF16), v, preferred_element_type=F32)
        m_scr[h] = m_new

    @pl.when(kv == pl.num_programs(3) - 1)
    def _():
        for h in range(Q_PER_KV):
            sl = slice(h * HEAD_DIM, (h + 1) * HEAD_DIM)
            o = acc_scr[h] / l_scr[h]
            o_ref[:, sl] = (o * g_ref[:, sl].astype(F32)).astype(o_ref.dtype)


def _attn_noshift_kernel(q_ref, k_ref, v_ref, g_ref, o_ref):
    tq = q_ref.shape[0]
    q = jnp.concatenate([q_ref[:, h * HEAD_DIM:(h + 1) * HEAD_DIM] for h in range(Q_PER_KV)], axis=0)
    s = lax.dot_general(q, k_ref[...], (((1,), (1,)), ((), ())), preferred_element_type=F32)
    p = jnp.exp2(s).astype(BF16)
    v = v_ref[...]
    v_ext = jnp.concatenate([v, jnp.ones_like(v)], axis=1)
    a = jnp.dot(p, v_ext, preferred_element_type=F32)
    for h in range(Q_PER_KV):
        sl = slice(h * HEAD_DIM, (h + 1) * HEAD_DIM)
        ah = a[h * tq:(h + 1) * tq]
        o = ah[:, :HEAD_DIM] / ah[:, HEAD_DIM:]
        o_ref[:, sl] = (o * g_ref[:, sl].astype(F32)).astype(o_ref.dtype)


def _attention_noshift(q, k_all, v_all, gate, tq):
    b, l, aw = q.shape
    s_len, kvw = k_all.shape[1], k_all.shape[2]
    kvh = kvw // HEAD_DIM
    gw = Q_PER_KV * HEAD_DIM
    assert aw == kvh * gw
    return pl.pallas_call(
        _attn_noshift_kernel,
        grid=(b, kvh, l // tq),
        in_specs=[pl.BlockSpec((None, tq, gw), lambda bi, h, qi: (bi, qi, h)),
                  pl.BlockSpec((None, s_len, HEAD_DIM), lambda bi, h, qi: (bi, 0, h)),
                  pl.BlockSpec((None, s_len, HEAD_DIM), lambda bi, h, qi: (bi, 0, h)),
                  pl.BlockSpec((None, tq, gw), lambda bi, h, qi: (bi, qi, h))],
        out_specs=pl.BlockSpec((None, tq, gw), lambda bi, h, qi: (bi, qi, h)),
        out_shape=jax.ShapeDtypeStruct((b, l, aw), BF16),
        compiler_params=_params("parallel", "parallel", "parallel"),
        name="attention_noshift",
    )(q, k_all, v_all, gate)


def _attention(q, k_all, v_all, gate, tq, tk):
    b, l, aw = q.shape
    s_len, kvw = k_all.shape[1], k_all.shape[2]
    kvh = kvw // HEAD_DIM
    gw = Q_PER_KV * HEAD_DIM
    assert aw == kvh * gw
    return pl.pallas_call(
        _attn_kernel,
        grid=(b, kvh, l // tq, s_len // tk),
        in_specs=[pl.BlockSpec((None, tq, gw), lambda bi, h, qi, ki: (bi, qi, h)),
                  pl.BlockSpec((None, tk, HEAD_DIM), lambda bi, h, qi, ki: (bi, ki, h)),
                  pl.BlockSpec((None, tk, HEAD_DIM), lambda bi, h, qi, ki: (bi, ki, h)),
                  pl.BlockSpec((None, tq, gw), lambda bi, h, qi, ki: (bi, qi, h))],
        out_specs=pl.BlockSpec((None, tq, gw), lambda bi, h, qi, ki: (bi, qi, h)),
        out_shape=jax.ShapeDtypeStruct((b, l, aw), BF16),
        scratch_shapes=[pltpu.VMEM((Q_PER_KV, tq, HEAD_DIM), F32)] * 3,
        compiler_params=_params("parallel", "parallel", "parallel", "arbitrary"),
        name="attention",
    )(q, k_all, v_all, gate)


def _s5_param_kernel(reverse, lr_ref, li_ref, ldt_ref, b1_ref, bsw_ref, c1_ref, csw_ref, k0_ref,
                     w_ref, vt_ref, kt_ref, a_ref):
    t = S5_CHUNK
    lr, li = lr_ref[...], li_ref[...]
    dt = jnp.exp(ldt_ref[...])
    lane = lax.broadcasted_iota(jnp.int32, (1, LANES), 1)
    neg_first = jnp.where(lane < 64, -1.0, 1.0).astype(F32)
    jj = lax.broadcasted_iota(jnp.int32, (S5_POW_ROWS, LANES), 0).astype(F32)
    mag = jnp.exp(lr * dt * jj)
    ang = li * dt * jj
    p_re = mag * jnp.cos(ang)
    p_im = mag * jnp.sin(ang)

    a_re, a_im = p_re[1:2], p_im[1:2]
    den = lr * lr + li * li
    n_re = a_re - 1.0
    coef_re = (n_re * lr + a_im * li) / den
    coef_im = (a_im * lr - n_re * li) / den
    b1 = b1_ref[...]
    b2 = bsw_ref[...] * neg_first
    bb1 = coef_re * b1 + coef_im * b2
    bb2 = coef_re * b2 - coef_im * b1

    c1 = c1_ref[...] * (-neg_first)
    csw = csw_ref[...]

    xs = []
    for j in range(t):
        x_j = p_re[j:j + 1] * bb1 + p_im[j:j + 1] * bb2
        xs.append(x_j)
        blk = (t - 1 - j) if not reverse else j
        w_ref[blk * 16:(blk + 1) * 16, :] = x_j
        m = (j + 1) if not reverse else (t - j)
        vt_ref[j * 16:(j + 1) * 16, :] = p_re[m:m + 1] * c1 - p_im[m:m + 1] * csw
    x_all = jnp.concatenate(xs, axis=0)
    kt = lax.dot_general(x_all, c1, (((1,), (1,)), ((), ())),
                         precision=lax.Precision.HIGHEST, preferred_element_type=F32)
    kt_ref[...] = kt
    kt_ref[0:16, :] = kt[0:16] + k0_ref[...]
    a_ref[...] = jnp.zeros(a_ref.shape, F32)
    a_ref[0:1, :] = p_re[t:t + 1]
    a_ref[1:2, :] = p_im[t:t + 1] * neg_first


def _s5_params(prm, reverse, k0):
    lam_re, lam_im, log_dt, b_re, b_im, c_re, c_im = prm
    g, p = lam_re.shape
    i = b_re.shape[2]
    assert 2 * p == LANES and i * S5_CHUNK == 256

    def dup(a):
        return jnp.concatenate([a, a], axis=-1).reshape(g, 1, LANES).astype(F32)

    bt_re, bt_im = jnp.swapaxes(b_re, 1, 2).astype(F32), jnp.swapaxes(b_im, 1, 2).astype(F32)
    b1 = jnp.concatenate([bt_re, bt_im], axis=-1)
    bsw = jnp.concatenate([bt_im, bt_re], axis=-1)
    c1 = jnp.concatenate([c_re, c_im], axis=-1).astype(F32)
    csw = jnp.concatenate([c_im, c_re], axis=-1).astype(F32)
    ldt = jnp.broadcast_to(log_dt.astype(F32)[:, None, None], (g, 1, LANES))
    row = pl.BlockSpec((None, 1, LANES), lambda gi: (gi, 0, 0))
    mat = pl.BlockSpec((None, i, LANES), lambda gi: (gi, 0, 0))
    big = pl.BlockSpec((None, 256, LANES), lambda gi: (gi, 0, 0))
    return pl.pallas_call(
        functools.partial(_s5_param_kernel, reverse),
        grid=(g,),
        in_specs=[row, row, row, mat, mat, mat, mat, pl.BlockSpec((None, i, i), lambda gi: (gi, 0, 0))],
        out_specs=[big, big,
                   pl.BlockSpec((None, 256, i), lambda gi: (gi, 0, 0)),
                   pl.BlockSpec((None, SUBLANES, LANES), lambda gi: (gi, 0, 0))],
        out_shape=[jax.ShapeDtypeStruct((g, 256, LANES), F32),
                   jax.ShapeDtypeStruct((g, 256, LANES), F32),
                   jax.ShapeDtypeStruct((g, 256, i), F32),
                   jax.ShapeDtypeStruct((g, SUBLANES, LANES), F32)],
        compiler_params=_params("parallel"),
        name="s5_params",
    )(dup(lam_re), dup(lam_im), ldt, b1, bsw, c1, csw, k0)


def _toeplitz(kt_f, kt_b):
    g = kt_f.shape[0]
    t = S5_CHUNK
    i = kt_f.shape[2]
    kf = kt_f.reshape(g, t, i, i)
    kb = kt_b.reshape(g, t, i, i)
    s_idx = jnp.arange(t)[:, None]
    t_idx = jnp.arange(t)[None, :]
    lag = t_idx - s_idx
    f_part = kf[:, jnp.clip(lag, 0, t - 1)]
    b_part = kb[:, jnp.clip(-lag, 0, t - 1)]
    full = jnp.where((lag > 0)[None, :, :, None, None], f_part, b_part)
    return jnp.transpose(full, (0, 1, 3, 2, 4)).reshape(g, t * i, t * i)


def _slot_transpose8(vs, slot):
    vs = list(vs)
    for d in (4, 2, 1):
        keep = (slot & d) == 0
        new = list(vs)
        for r in range(8):
            if r & d == 0:
                a, b = vs[r], vs[r + d]
                new[r] = jnp.where(keep, a, pltpu.roll(b, 16 * d, 1))
                new[r + d] = jnp.where(keep, pltpu.roll(a, LANES - 16 * d, 1), b)
        vs = new
    return vs


def _gather_chunks(src_ref, dst_scr, n, rb):
    t = S5_CHUNK
    for r0 in range(0, n, rb):
        slot = lax.broadcasted_iota(jnp.int32, (rb, LANES), 1) // 16
        for hi in range(2):
            pieces = [src_ref[pl.ds(r0 * t + hi * 8 + r, rb, stride=t), :] for r in range(8)]
            outs = _slot_transpose8(pieces, slot)
            for gi in range(8):
                dst_scr[gi, r0:r0 + rb, hi * LANES:(hi + 1) * LANES] = outs[gi].astype(BF16)


def _s5_kernel(n, nc, rb, ux_ref, uc_ref, toep_ref, wf_ref, wb_ref, vf_ref, vb_ref, af_ref, ab_ref,
               y_ref, u_scr, ucx_scr, wf_scr, wfs_scr, wb_scr, wbs_scr, hf_scr, gb_scr):
    t = S5_CHUNK
    ntot = n + nc
    _gather_chunks(ux_ref, u_scr, n, rb)
    _gather_chunks(uc_ref, ucx_scr, nc, min(rb, nc))

    for gi in range(8):
        ls = slice(gi * LANES, (gi + 1) * LANES)
        ug, ucg = u_scr[gi], ucx_scr[gi]
        wf, wb = wf_ref[gi], wb_ref[gi]
        f_c = jnp.dot(ucg, wf, preferred_element_type=F32)
        f_x = jnp.dot(ug, wf, preferred_element_type=F32)
        b_x = jnp.dot(ug, wb, preferred_element_type=F32)
        b_c = jnp.dot(ucg, wb, preferred_element_type=F32)
        wf_scr[0:nc, ls] = f_c
        wf_scr[nc:ntot, ls] = f_x
        wb_scr[0:n, ls] = b_x
        wb_scr[n:ntot, ls] = b_c
        wfs_scr[0:nc, ls] = pltpu.roll(f_c, 64, 1)
        wfs_scr[nc:ntot, ls] = pltpu.roll(f_x, 64, 1)
        wbs_scr[0:n, ls] = pltpu.roll(b_x, 64, 1)
        wbs_scr[n:ntot, ls] = pltpu.roll(b_c, 64, 1)

    a1f = jnp.concatenate([af_ref[gi, 0:1, :] for gi in range(8)], axis=1)
    a2f = jnp.concatenate([af_ref[gi, 1:2, :] for gi in range(8)], axis=1)
    a1b = jnp.concatenate([ab_ref[gi, 0:1, :] for gi in range(8)], axis=1)
    a2b = jnp.concatenate([ab_ref[gi, 1:2, :] for gi in range(8)], axis=1)

    def step(c, carry):
        h, hs, g, gs = carry
        hf_scr[pl.ds(c, 1), :] = h
        r = ntot - 1 - c
        gb_scr[pl.ds(r, 1), :] = g
        h_new = a1f * h + a2f * hs + wf_scr[pl.ds(c, 1), :]
        hs_new = a1f * hs - a2f * h + wfs_scr[pl.ds(c, 1), :]
        g_new = a1b * g + a2b * gs + wb_scr[pl.ds(r, 1), :]
        gs_new = a1b * gs - a2b * g + wbs_scr[pl.ds(r, 1), :]
        return h_new, hs_new, g_new, gs_new

    zero = jnp.zeros((1, 8 * LANES), F32)
    lax.fori_loop(0, ntot, step, (zero, zero, zero, zero))

    nt = (((1,), (1,)), ((), ()))
    for r0 in range(0, n, rb):
        slot = lax.broadcasted_iota(jnp.int32, (rb, LANES), 1) // 16
        ys = []
        for gi in range(8):
            ls = slice(gi * LANES, (gi + 1) * LANES)
            y = jnp.dot(u_scr[gi, r0:r0 + rb, :], toep_ref[gi], preferred_element_type=F32)
            y += lax.dot_general(hf_scr[nc + r0:nc + r0 + rb, ls].astype(BF16), vf_ref[gi], nt,
                                 preferred_element_type=F32)
            y += lax.dot_general(gb_scr[r0:r0 + rb, ls].astype(BF16), vb_ref[gi], nt,
                                 preferred_element_type=F32)
            ys.append(y)
        for hi in range(2):
            outs = _slot_transpose8([ys[gi][:, hi * LANES:(hi + 1) * LANES] for gi in range(8)], slot)
            for r in range(8):
                y_ref[pl.ds(r0 * t + hi * 8 + r, rb, stride=t), :] = outs[r]


def _s5(u, uc, toep, wf, wb, vf, vb, af, ab):
    b, l, c = u.shape
    lc = uc.shape[1]
    t = S5_CHUNK
    n, nc = l // t, lc // t
    assert l % t == 0 and lc % t == 0 and c % LANES == 0 and n % SUBLANES == 0 and nc % SUBLANES == 0
    rb = _tile(n, 128, SUBLANES)
    ntot = n + nc
    gmat = lambda last: pl.BlockSpec((8, 256, last), lambda bi, ci: (ci, 0, 0))
    return pl.pallas_call(
        functools.partial(_s5_kernel, n, nc, rb),
        grid=(b, c // LANES),
        in_specs=[pl.BlockSpec((None, l, LANES), lambda bi, ci: (bi, 0, ci)),
                  pl.BlockSpec((None, lc, LANES), lambda bi, ci: (bi, 0, ci)),
                  gmat(256), gmat(LANES), gmat(LANES), gmat(LANES), gmat(LANES),
                  pl.BlockSpec((8, SUBLANES, LANES), lambda bi, ci: (ci, 0, 0)),
                  pl.BlockSpec((8, SUBLANES, LANES), lambda bi, ci: (ci, 0, 0))],
        out_specs=pl.BlockSpec((None, l, LANES), lambda bi, ci: (bi, 0, ci)),
        out_shape=jax.ShapeDtypeStruct((b, l, c), F32),
        scratch_shapes=[pltpu.VMEM((8, n, 256), BF16),
                        pltpu.VMEM((8, nc, 256), BF16),
                        pltpu.VMEM((ntot, 8 * LANES), F32),
                        pltpu.VMEM((ntot, 8 * LANES), F32),
                        pltpu.VMEM((ntot, 8 * LANES), F32),
                        pltpu.VMEM((ntot, 8 * LANES), F32),
                        pltpu.VMEM((ntot, 8 * LANES), F32),
                        pltpu.VMEM((ntot, 8 * LANES), F32)],
        compiler_params=_params("parallel", "parallel"),
        name="s5_scan",
    )(u, uc, toep, wf, wb, vf, vb, af, ab)


def _glu_kernel(nj, tn, y_ref, u_ref, d_ref, w_ref, b_ref, g_ref, o_ref, v32_scr, vb_scr):
    j = pl.program_id(1)

    @pl.when(j == 0)
    def _():
        v = jax.nn.gelu(y_ref[...] + d_ref[...] * u_ref[...])
        vb_scr[...] = v.astype(BF16)
        for jj in range(nj):
            v32_scr[jj] = v[:, jj * tn:(jj + 1) * tn]

    acc = jnp.dot(vb_scr[...], w_ref[...], preferred_element_type=F32) + b_ref[...]
    o_ref[...] = (v32_scr[j] * _sigmoid(acc) * g_ref[...].astype(F32)).astype(o_ref.dtype)


def _glu(y, u, d_skip, w_bf, b_glu, gate, tm):
    m, c = y.shape
    tn = _tile(c, 512, LANES)
    nj = c // tn
    return pl.pallas_call(
        functools.partial(_glu_kernel, nj, tn),
        grid=(m // tm, nj),
        in_specs=[pl.BlockSpec((tm, c), lambda i, j: (i, 0)),
                  pl.BlockSpec((tm, c), lambda i, j: (i, 0)),
                  pl.BlockSpec((1, c), lambda i, j: (0, 0)),
                  pl.BlockSpec((c, tn), lambda i, j: (0, j)),
                  pl.BlockSpec((1, tn), lambda i, j: (0, j)),
                  pl.BlockSpec((tm, tn), lambda i, j: (i, j))],
        out_specs=pl.BlockSpec((tm, tn), lambda i, j: (i, j)),
        out_shape=jax.ShapeDtypeStruct((m, c), BF16),
        scratch_shapes=[pltpu.VMEM((nj, tm, tn), F32), pltpu.VMEM((tm, c), BF16)],
        compiler_params=_params("parallel", "arbitrary"),
        name="ssm_glu",
    )(y, u, d_skip.reshape(1, c), w_bf, b_glu.reshape(1, c), gate)


def _outproj_kernel(a_ref, s_ref, wa_ref, ws_ref, x_ref, g_ref, o_ref):
    acc = jnp.dot(a_ref[...], wa_ref[...], preferred_element_type=F32)
    acc += jnp.dot(s_ref[...], ws_ref[...], preferred_element_type=F32)
    o_ref[...] = x_ref[...] + g_ref[...] * acc


def _outproj(lhs_a, lhs_s, col_s, w_bf, x2, gate, rows_per_batch, tm, name):
    m = lhs_a.shape[0]
    kh = w_bf.shape[0] // 2
    d = w_bf.shape[1]
    tn = _tile(d, 512, LANES)
    tpb = rows_per_batch // tm
    assert rows_per_batch % tm == 0 and lhs_a.shape[1] % kh == 0 and lhs_s.shape[1] % kh == 0
    return pl.pallas_call(
        _outproj_kernel,
        grid=(m // tm, d // tn),
        in_specs=[pl.BlockSpec((tm, kh), lambda i, j: (i, 0)),
                  pl.BlockSpec((tm, kh), lambda i, j: (i, col_s)),
                  pl.BlockSpec((kh, tn), lambda i, j: (0, j)),
                  pl.BlockSpec((kh, tn), lambda i, j: (1, j)),
                  pl.BlockSpec((tm, tn), lambda i, j: (i, j)),
                  pl.BlockSpec((None, 1, tn), lambda i, j: (i // tpb, 0, j))],
        out_specs=pl.BlockSpec((tm, tn), lambda i, j: (i, j)),
        out_shape=jax.ShapeDtypeStruct((m, d), F32),
        compiler_params=_params("parallel", "parallel"),
        name=name,
    )(lhs_a, lhs_s, w_bf, w_bf, x2, gate)


def _inproj1_kernel(tpb, x_ref, xp_ref, xq_ref, nw_ref, sh_ref, sc_ref, wb_ref, wc_ref, wx_ref, wg_ref,
                    cw_ref, cb_ref, o_ref, xn_scr):
    i = pl.program_id(0)
    tm = x_ref.shape[0]

    @pl.when(pl.program_id(1) == 0)
    def _():
        nw, sh, sc = nw_ref[...], sh_ref[...], sc_ref[...]
        xn_scr[0:HALO, :] = _modulated(xp_ref[...], nw, sh, sc).astype(BF16)
        xn_scr[HALO:HALO + tm, :] = _modulated(x_ref[...], nw, sh, sc).astype(BF16)
        xn_scr[HALO + tm:, :] = _modulated(xq_ref[...], nw, sh, sc).astype(BF16)

    xn_all = xn_scr[...]
    xn = xn_scr[HALO:HALO + tm, :]
    c_gate = jnp.dot(xn_all, wc_ref[...], preferred_element_type=F32)
    xin = jnp.dot(xn_all, wx_ref[...], preferred_element_type=F32)
    b_gate = jnp.dot(xn, wb_ref[...], preferred_element_type=F32)
    g = jnp.dot(xn, wg_ref[...], preferred_element_type=F32)

    y = c_gate * xin
    rows = y.shape[0]
    row = lax.broadcasted_iota(jnp.int32, y.shape, 0)
    lo = jnp.where((i % tpb) == 0, HALO, 0)
    hi = jnp.where((i % tpb) == tpb - 1, HALO + tm, rows)
    y = jnp.where((row >= lo) & (row < hi), y, 0.0)
    y_prev = pltpu.roll(y, 1, 0)[HALO:HALO + tm]
    y_next = pltpu.roll(y, rows - 1, 0)[HALO:HALO + tm]
    conv = cb_ref[...] + y_prev * cw_ref[0:1, :] + y[HALO:HALO + tm] * cw_ref[1:2, :] + y_next * cw_ref[2:3, :]
    o_ref[...] = (b_gate * conv * _silu(g)).astype(o_ref.dtype)


def _inproj1(x2, norm_w, shift, scale, w_bf, conv_w, conv_b, rows_per_batch, tm):
    m, d = x2.shape
    c = w_bf.shape[1] // 4
    tn = _tile(c, 256, LANES)
    nj = c // tn
    tpb = rows_per_batch // tm
    hb = tm // HALO
    nhb = m // HALO
    assert rows_per_batch % tm == 0 and tm % HALO == 0

    def wspec(p):
        return pl.BlockSpec((d, tn), lambda i, j: (0, p * nj + j))

    return pl.pallas_call(
        functools.partial(_inproj1_kernel, tpb),
        grid=(m // tm, nj),
        in_specs=[pl.BlockSpec((tm, d), lambda i, j: (i, 0)),
                  pl.BlockSpec((HALO, d), lambda i, j: (jnp.maximum(i * hb - 1, 0), 0)),
                  pl.BlockSpec((HALO, d), lambda i, j: (jnp.minimum((i + 1) * hb, nhb - 1), 0)),
                  pl.BlockSpec((1, d), lambda i, j: (0, 0)),
                  pl.BlockSpec((None, 1, d), lambda i, j: (i // tpb, 0, 0)),
                  pl.BlockSpec((None, 1, d), lambda i, j: (i // tpb, 0, 0)),
                  wspec(0), wspec(1), wspec(2), wspec(3),
                  pl.BlockSpec((CONV_K, tn), lambda i, j: (0, j)),
                  pl.BlockSpec((1, tn), lambda i, j: (0, j))],
        out_specs=pl.BlockSpec((tm, tn), lambda i, j: (i, j)),
        out_shape=jax.ShapeDtypeStruct((m, c), BF16),
        scratch_shapes=[pltpu.VMEM((tm + 2 * HALO, d), BF16)],
        compiler_params=_params("parallel", "arbitrary"),
        name="inproj1",
    )(x2, x2, x2, norm_w.reshape(1, d), shift, scale, w_bf, w_bf, w_bf, w_bf, conv_w, conv_b.reshape(1, c))


def _rope_tables(n_tok):
    rows = n_tok // GRID_W
    r, col = jnp.meshgrid(jnp.arange(rows, dtype=F32), jnp.arange(GRID_W, dtype=F32), indexing="ij")
    axis_dim = HEAD_DIM // 2
    inv_freq = ROPE_THETA ** (-jnp.arange(0, axis_dim, 2, dtype=F32) / axis_dim)
    ar = r.reshape(-1)[:, None] * inv_freq
    ac = col.reshape(-1)[:, None] * inv_freq
    cos_t = jnp.concatenate([jnp.cos(ar), jnp.cos(ar), jnp.cos(ac), jnp.cos(ac)], axis=1)
    sin_t = jnp.concatenate([-jnp.sin(ar), jnp.sin(ar), -jnp.sin(ac), jnp.sin(ac)], axis=1)
    return cos_t, sin_t


def kernel(x, c, ctx, c_ctx, l0_norm_w, l0_w_mod, l0_b_mod, l0_w_in, l0_q_norm_w, l0_k_norm_w, l0_fwd_lam_re, l0_fwd_lam_im, l0_fwd_log_dt, l0_fwd_b_re, l0_fwd_b_im, l0_fwd_c_re, l0_fwd_c_im, l0_bwd_lam_re, l0_bwd_lam_im, l0_bwd_log_dt, l0_bwd_b_re, l0_bwd_b_im, l0_bwd_c_re, l0_bwd_c_im, l0_ssm_d, l0_w_glu, l0_b_glu, l0_w_out, l1_norm_w, l1_w_mod, l1_b_mod, l1_w_in, l1_conv_w, l1_conv_b, l1_w_out):
    b, l, d = x.shape
    lc = ctx.shape[1]
    ssm_w = l0_ssm_d.shape[0]
    att_w = l0_w_out.shape[0] - ssm_w
    kv_w = att_w // Q_PER_KV
    widths = (att_w, kv_w, kv_w, att_w, ssm_w, ssm_w)
    assert sum(widths) == l0_w_in.shape[1]

    pad = (-(b + 1)) % 16
    cond = jnp.concatenate([c, c_ctx[None, :], jnp.zeros((pad, d), F32)], axis=0)
    mod0 = _ada_mod(cond, l0_w_mod, l0_b_mod)
    mod1 = _ada_mod(cond, l1_w_mod, l1_b_mod)
    shift0, scale0, gate0 = (mod0[:b, k * d:(k + 1) * d].reshape(b, 1, d) for k in range(3))
    shift0c, scale0c = (jnp.broadcast_to(mod0[b, k * d:(k + 1) * d].reshape(1, 1, d), (b, 1, d)) for k in range(2))
    shift1, scale1, gate1 = (mod1[:b, k * d:(k + 1) * d].reshape(b, 1, d) for k in range(3))

    x2 = x.reshape(b * l, d)
    ctx2 = ctx.reshape(b * lc, d)
    w_in0 = l0_w_in.astype(BF16)

    cos_t, sin_t = _rope_tables(l)
    tm = _tile(l, 512, SUBLANES)
    q, k, v, g_att, u, g_ssm = _inproj0(x2, l0_norm_w, shift0, scale0, w_in0, cos_t, sin_t,
                                        l0_q_norm_w, l0_k_norm_w, widths, l, tm)
    tmc = _tile(lc, 256, SUBLANES)
    ones_t, zeros_t = jnp.ones((lc, HEAD_DIM), F32), jnp.zeros((lc, HEAD_DIM), F32)
    _, kc, vc, _, uc, _ = _inproj0(ctx2, l0_norm_w, shift0c, scale0c, w_in0, ones_t, zeros_t,
                                   l0_q_norm_w, l0_k_norm_w, widths, lc, tmc)

    k_all = jnp.concatenate([kc.reshape(b, lc, kv_w), k.reshape(b, l, kv_w)], axis=1)
    v_all = jnp.concatenate([vc.reshape(b, lc, kv_w), v.reshape(b, l, kv_w)], axis=1)
    tq = _tile(l, 512, SUBLANES)
    tk = _tile(l + lc, 1408, LANES)
    score_bound = HEAD_DIM ** 0.5 * jnp.max(jnp.abs(l0_q_norm_w)) * jnp.max(jnp.abs(l0_k_norm_w))
    att = lax.cond(score_bound < SCORE_BOUND_NO_SHIFT,
                   functools.partial(_attention_noshift, tq=_tile(l, 256, SUBLANES)),
                   functools.partial(_attention, tq=tq, tk=tk),
                   q.reshape(b, l, att_w), k_all, v_all, g_att.reshape(b, l, att_w))

    fwd = (l0_fwd_lam_re, l0_fwd_lam_im, l0_fwd_log_dt, l0_fwd_b_re, l0_fwd_b_im, l0_fwd_c_re, l0_fwd_c_im)
    bwd = (l0_bwd_lam_re, l0_bwd_lam_im, l0_bwd_log_dt, l0_bwd_b_re, l0_bwd_b_im, l0_bwd_c_re, l0_bwd_c_im)
    n_in = l0_fwd_b_re.shape[2]
    wf, vf, ktf, af = _s5_params(fwd, False, jnp.zeros((ssm_w // n_in, n_in, n_in), F32))
    wb, vb, ktb, ab = _s5_params(bwd, True, ktf[:, :n_in, :])
    toep = _toeplitz(ktf, ktb).astype(BF16)
    y = _s5(u.reshape(b, l, ssm_w), uc.reshape(b, lc, ssm_w), toep,
            wf.astype(BF16), wb.astype(BF16), vf.astype(BF16), vb.astype(BF16), af, ab)
    ssm = _glu(y.reshape(b * l, ssm_w), u, l0_ssm_d, l0_w_glu.astype(BF16), l0_b_glu, g_ssm, tm)

    tmo = _tile(l, 1024, SUBLANES)
    assert att_w == ssm_w
    h1 = _outproj(att.reshape(b * l, att_w), ssm, 0, l0_w_out.astype(BF16), x2, gate0, l, tmo, "outproj0")

    mix1 = _inproj1(h1, l1_norm_w, shift1, scale1, l1_w_in.astype(BF16), l1_conv_w, l1_conv_b, l, tm)
    h2 = _outproj(mix1, mix1, 1, l1_w_out.astype(BF16), h1, gate1, l, tmo, "outproj1")
    return h2.reshape(b, l, d)
```

```python
import functools
import math

import jax
import jax.numpy as jnp
from jax import lax
from jax.experimental import pallas as pl
from jax.experimental.pallas import tpu as pltpu

F32 = jnp.float32
BF16 = jnp.bfloat16

EPS = 1e-6
HEAD_DIM = 128
Q_PER_KV = 4
GRID_W = 64
ROPE_THETA = 10000.0
CONV_K = 3
Q_SCALE = HEAD_DIM ** -0.5 * math.log2(math.e)
SCORE_BOUND_NO_SHIFT = 40.0

LANES = 128
SUBLANES = 8
HALO = 16
S5_CHUNK = 16
S5_POW_ROWS = 32
S5_TAB_ROWS = 10 * SUBLANES
VMEM_LIMIT_BYTES = 56 * 1024 * 1024


def _params(*semantics):
    return pltpu.CompilerParams(dimension_semantics=semantics, vmem_limit_bytes=VMEM_LIMIT_BYTES)


def _tile(dim, pref, mult):
    if dim <= pref:
        return dim
    t = (pref // mult) * mult
    while t > mult and dim % t:
        t -= mult
    assert dim % t == 0, (dim, pref, mult)
    return t


def _sigmoid(x):
    return 1.0 / (1.0 + jnp.exp(-x))


def _silu(x):
    return x * _sigmoid(x)


def _ada_kernel(c_ref, w_ref, b_ref, o_ref):
    s = _silu(c_ref[...]).astype(BF16)
    o_ref[...] = jnp.dot(s, w_ref[...].astype(BF16), preferred_element_type=F32) + b_ref[...]


def _ada_mod(cond, w_mod, b_mod):
    rows, d = cond.shape
    n = w_mod.shape[1]
    tn = _tile(n, 512, LANES)
    return pl.pallas_call(
        _ada_kernel,
        grid=(n // tn,),
        in_specs=[pl.BlockSpec((rows, d), lambda j: (0, 0)),
                  pl.BlockSpec((d, tn), lambda j: (0, j)),
                  pl.BlockSpec((1, tn), lambda j: (0, j))],
        out_specs=pl.BlockSpec((rows, tn), lambda j: (0, j)),
        out_shape=jax.ShapeDtypeStruct((rows, n), F32),
        compiler_params=_params("parallel"),
        name="ada_mod",
    )(cond, w_mod, b_mod.reshape(1, n))


def _modulated(x, nw, shift, scale):
    ms = jnp.mean(x * x, axis=-1, keepdims=True)
    return (x * lax.rsqrt(ms + EPS) * nw) * (1.0 + scale) + shift


def _norm_rope_heads(acc, nw, cos, sin, out_scale, o_ref):
    lane = lax.broadcasted_iota(jnp.int32, (acc.shape[0], HEAD_DIM), 1)
    first_half = (lane % 64) < 32
    for h in range(acc.shape[1] // HEAD_DIM):
        a = acc[:, h * HEAD_DIM:(h + 1) * HEAD_DIM]
        ms = jnp.mean(a * a, axis=-1, keepdims=True)
        a = a * lax.rsqrt(ms + EPS) * nw
        swapped = jnp.where(first_half, pltpu.roll(a, HEAD_DIM - 32, 1), pltpu.roll(a, 32, 1))
        o_ref[:, h * HEAD_DIM:(h + 1) * HEAD_DIM] = ((a * cos + swapped * sin) * out_scale).astype(o_ref.dtype)


def _inproj0_kernel(bounds, x_ref, nw_ref, sh_ref, sc_ref, w_ref, cos_ref, sin_ref, qn_ref, kn_ref,
                    q_ref, k_ref, v_ref, ga_ref, u_ref, gs_ref, xn_scr, acc_scr):
    j = pl.program_id(1)
    e_q, e_k, e_v, e_ga, e_u = bounds

    def dot():
        return jnp.dot(xn_scr[...], w_ref[...], preferred_element_type=F32)

    @pl.when(j == 0)
    def _():
        xn_scr[...] = _modulated(x_ref[...], nw_ref[...], sh_ref[...], sc_ref[...]).astype(BF16)
        acc_scr[0] = dot()

    @pl.when((j >= 1) & (j <= e_q))
    def _():
        _norm_rope_heads(acc_scr[(j - 1) % 2], qn_ref[...], cos_ref[...], sin_ref[...], Q_SCALE, q_ref)
        acc_scr[j % 2] = dot()

    @pl.when(j == e_k)
    def _():
        _norm_rope_heads(acc_scr[(j - 1) % 2], kn_ref[...], cos_ref[...], sin_ref[...], 1.0, k_ref)
        v_ref[...] = dot().astype(BF16)

    @pl.when((j >= e_v) & (j < e_ga))
    def _():
        ga_ref[...] = _silu(dot()).astype(BF16)

    @pl.when((j >= e_ga) & (j < e_u))
    def _():
        u_ref[...] = dot()

    @pl.when(j >= e_u)
    def _():
        gs_ref[...] = _silu(dot()).astype(BF16)


def _inproj0(x2, norm_w, shift, scale, w_bf, cos_t, sin_t, qn, kn, widths, rows_per_batch, tm):
    m, d = x2.shape
    n = w_bf.shape[1]
    tn = _tile(min(widths), 512, LANES)
    assert all(w % tn == 0 for w in widths) and sum(widths) == n
    ends, off = [], 0
    for w in widths:
        off += w
        ends.append(off // tn)
    starts = [0] + ends[:-1]
    assert ends[1] - starts[1] == 1 and ends[2] - starts[2] == 1, "k and v must be one column tile each"
    tpb = rows_per_batch // tm
    assert rows_per_batch % tm == 0 and m % tm == 0

    def out_spec(p):
        lo, hi = starts[p], ends[p]
        late = 1 if p == 0 else 0
        return pl.BlockSpec((tm, tn), lambda i, j: (i, jnp.clip(j - late - lo, 0, hi - lo - 1)))

    dts = (BF16, BF16, BF16, BF16, F32, BF16)
    return pl.pallas_call(
        functools.partial(_inproj0_kernel, tuple(ends[:5])),
        grid=(m // tm, n // tn),
        in_specs=[pl.BlockSpec((tm, d), lambda i, j: (i, 0)),
                  pl.BlockSpec((1, d), lambda i, j: (0, 0)),
                  pl.BlockSpec((None, 1, d), lambda i, j: (i // tpb, 0, 0)),
                  pl.BlockSpec((None, 1, d), lambda i, j: (i // tpb, 0, 0)),
                  pl.BlockSpec((d, tn), lambda i, j: (0, j)),
                  pl.BlockSpec((tm, HEAD_DIM), lambda i, j: (i % tpb, 0)),
                  pl.BlockSpec((tm, HEAD_DIM), lambda i, j: (i % tpb, 0)),
                  pl.BlockSpec((1, HEAD_DIM), lambda i, j: (0, 0)),
                  pl.BlockSpec((1, HEAD_DIM), lambda i, j: (0, 0))],
        out_specs=[out_spec(p) for p in range(6)],
        out_shape=[jax.ShapeDtypeStruct((m, w), dt) for w, dt in zip(widths, dts)],
        scratch_shapes=[pltpu.VMEM((tm, d), BF16), pltpu.VMEM((2, tm, tn), F32)],
        compiler_params=_params("parallel", "arbitrary"),
        name="inproj0",
    )(x2, norm_w.reshape(1, d), shift, scale, w_bf, cos_t, sin_t, qn.reshape(1, HEAD_DIM), kn.reshape(1, HEAD_DIM))


def _attn_kernel(q_ref, k_ref, v_ref, g_ref, o_ref, m_scr, l_scr, acc_scr):
    kv = pl.program_id(3)

    @pl.when(kv == 0)
    def _():
        m_scr[...] = jnp.full(m_scr.shape, -jnp.inf, F32)
        l_scr[...] = jnp.zeros(l_scr.shape, F32)
        acc_scr[...] = jnp.zeros(acc_scr.shape, F32)

    k = k_ref[...]
    v = v_ref[...]
    for h in range(Q_PER_KV):
        qh = q_ref[:, h * HEAD_DIM:(h + 1) * HEAD_DIM]
        s = lax.dot_general(qh, k, (((1,), (1,)), ((), ())), preferred_element_type=F32)
        m_prev = m_scr[h]
        m_new = jnp.maximum(m_prev, jnp.max(s, axis=1, keepdims=True))
        alpha = jnp.exp2(m_prev - m_new)
        p = jnp.exp2(s - m_new[:, :1])
        l_scr[h] = alpha * l_scr[h] + jnp.sum(p, axis=1, keepdims=True)
        acc_scr[h] = alpha * acc_scr[h] + jnp.dot(p.astype(BF16), v, preferred_element_type=F32)
        m_scr[h] = m_new

    @pl.when(kv == pl.num_programs(3) - 1)
    def _():
        for h in range(Q_PER_KV):
            sl = slice(h * HEAD_DIM, (h + 1) * HEAD_DIM)
            o = acc_scr[h] / l_scr[h]
            o_ref[:, sl] = (o * g_ref[:, sl].astype(F32)).astype(o_ref.dtype)


def _attn_noshift_kernel(q_ref, k_ref, v_ref, g_ref, o_ref):
    tq = q_ref.shape[0]
    q = jnp.concatenate([q_ref[:, h * HEAD_DIM:(h + 1) * HEAD_DIM] for h in range(Q_PER_KV)], axis=0)
    s = lax.dot_general(q, k_ref[...], (((1,), (1,)), ((), ())), preferred_element_type=F32)
    p = jnp.exp2(s).astype(BF16)
    v = v_ref[...]
    v_ext = jnp.concatenate([v, jnp.ones_like(v)], axis=1)
    a = jnp.dot(p, v_ext, preferred_element_type=F32)
    for h in range(Q_PER_KV):
        sl = slice(h * HEAD_DIM, (h + 1) * HEAD_DIM)
        ah = a[h * tq:(h + 1) * tq]
        o = ah[:, :HEAD_DIM] / ah[:, HEAD_DIM:]
        o_ref[:, sl] = (o * g_ref[:, sl].astype(F32)).astype(o_ref.dtype)


def _attention_noshift(q, k_all, v_all, gate, tq):
    b, l, aw = q.shape
    s_len, kvw = k_all.shape[1], k_all.shape[2]
    kvh = kvw // HEAD_DIM
    gw = Q_PER_KV * HEAD_DIM
    assert aw == kvh * gw
    return pl.pallas_call(
        _attn_noshift_kernel,
        grid=(b, kvh, l // tq),
        in_specs=[pl.BlockSpec((None, tq, gw), lambda bi, h, qi: (bi, qi, h)),
                  pl.BlockSpec((None, s_len, HEAD_DIM), lambda bi, h, qi: (bi, 0, h)),
                  pl.BlockSpec((None, s_len, HEAD_DIM), lambda bi, h, qi: (bi, 0, h)),
                  pl.BlockSpec((None, tq, gw), lambda bi, h, qi: (bi, qi, h))],
        out_specs=pl.BlockSpec((None, tq, gw), lambda bi, h, qi: (bi, qi, h)),
        out_shape=jax.ShapeDtypeStruct((b, l, aw), BF16),
        compiler_params=_params("parallel", "parallel", "parallel"),
        name="attention_noshift",
    )(q, k_all, v_all, gate)


def _attention(q, k_all, v_all, gate, tq):
    tk = _tile(k_all.shape[1], 1408, LANES)
    b, l, aw = q.shape
    s_len, kvw = k_all.shape[1], k_all.shape[2]
    kvh = kvw // HEAD_DIM
    gw = Q_PER_KV * HEAD_DIM
    assert aw == kvh * gw
    return pl.pallas_call(
        _attn_kernel,
        grid=(b, kvh, l // tq, s_len // tk),
        in_specs=[pl.BlockSpec((None, tq, gw), lambda bi, h, qi, ki: (bi, qi, h)),
                  pl.BlockSpec((None, tk, HEAD_DIM), lambda bi, h, qi, ki: (bi, ki, h)),
                  pl.BlockSpec((None, tk, HEAD_DIM), lambda bi, h, qi, ki: (bi, ki, h)),
                  pl.BlockSpec((None, tq, gw), lambda bi, h, qi, ki: (bi, qi, h))],
        out_specs=pl.BlockSpec((None, tq, gw), lambda bi, h, qi, ki: (bi, qi, h)),
        out_shape=jax.ShapeDtypeStruct((b, l, aw), BF16),
        scratch_shapes=[pltpu.VMEM((Q_PER_KV, tq, HEAD_DIM), F32)] * 3,
        compiler_params=_params("parallel", "parallel", "parallel", "arbitrary"),
        name="attention",
    )(q, k_all, v_all, gate)


def _s5_param_kernel(reverse, lr_ref, li_ref, ldt_ref, b1_ref, bsw_ref, c1_ref, csw_ref, k0_ref,
                     w_ref, vt_ref, kt_ref, a_ref):
    for gi in range(lr_ref.shape[0]):
        _s5_group_params(reverse, lr_ref.at[gi], li_ref.at[gi], ldt_ref.at[gi], b1_ref.at[gi], bsw_ref.at[gi],
                         c1_ref.at[gi], csw_ref.at[gi], k0_ref.at[gi],
                         w_ref.at[gi], vt_ref.at[gi], kt_ref.at[gi], a_ref.at[gi])


def _s5_group_params(reverse, lr_ref, li_ref, ldt_ref, b1_ref, bsw_ref, c1_ref, csw_ref, k0_ref,
                     w_ref, vt_ref, kt_ref, a_ref):
    t = S5_CHUNK
    lr, li = lr_ref[...], li_ref[...]
    dt = jnp.exp(ldt_ref[...])
    lane = lax.broadcasted_iota(jnp.int32, (1, LANES), 1)
    neg_first = jnp.where(lane < 64, -1.0, 1.0).astype(F32)
    jj = lax.broadcasted_iota(jnp.int32, (S5_POW_ROWS, LANES), 0).astype(F32)
    mag = jnp.exp(lr * dt * jj)
    ang = li * dt * jj
    p_re = mag * jnp.cos(ang)
    p_im = mag * jnp.sin(ang)

    a_re, a_im = p_re[1:2], p_im[1:2]
    den = lr * lr + li * li
    n_re = a_re - 1.0
    coef_re = (n_re * lr + a_im * li) / den
    coef_im = (a_im * lr - n_re * li) / den
    b1 = b1_ref[...]
    b2 = bsw_ref[...] * neg_first
    bb1 = coef_re * b1 + coef_im * b2
    bb2 = coef_re * b2 - coef_im * b1

    c1 = c1_ref[...] * (-neg_first)
    csw = csw_ref[...]

    xs = [None] * t
    for j in range(t):
        x_j = p_re[j:j + 1] * bb1 + p_im[j:j + 1] * bb2
        blk = (t - 1 - j) if not reverse else j
        xs[blk] = x_j
        w_ref[blk * 16:(blk + 1) * 16, 0:LANES] = x_j.astype(w_ref.dtype)
        w_ref[blk * 16:(blk + 1) * 16, LANES:2 * LANES] = pltpu.roll(x_j, LANES // 2, 1).astype(w_ref.dtype)
        m = (j + 1) if not reverse else (t - j)
        vt_ref[j * 16:(j + 1) * 16, :] = (p_re[m:m + 1] * c1 - p_im[m:m + 1] * csw).astype(vt_ref.dtype)
    x_all = jnp.concatenate(xs, axis=0)
    kt = lax.dot_general(x_all, c1, (((1,), (1,)), ((), ())),
                         precision=lax.Precision.HIGHEST, preferred_element_type=F32)
    kt_ref[...] = kt
    lag0 = 0 if reverse else (t - 1) * 16
    kt_ref[lag0:lag0 + 16, :] = kt[lag0:lag0 + 16] + k0_ref[...]

    r = lax.broadcasted_iota(jnp.int32, (SUBLANES, LANES), 0)
    last = SUBLANES - 1

    def table(k, expo, keep):
        e = expo.astype(F32) * float(t)
        mg = jnp.exp(lr * dt * e)
        an = li * dt * e
        a_ref[2 * k * SUBLANES:(2 * k + 1) * SUBLANES, :] = jnp.where(keep, mg * jnp.cos(an), 0.0)
        a_ref[(2 * k + 1) * SUBLANES:(2 * k + 2) * SUBLANES, :] = jnp.where(keep, mg * jnp.sin(an) * neg_first, 0.0)

    for k, d in enumerate((1, 2, 4)):
        table(k, jnp.full_like(r, d), (r <= last - d) if reverse else (r >= d))
    table(3, (last - r) if reverse else r, r >= 0)
    table(4, jnp.full_like(r, SUBLANES), r >= 0)


def _s5_params(prm, reverse, k0):
    lam_re, lam_im, log_dt, b_re, b_im, c_re, c_im = prm
    g, p = lam_re.shape
    i = b_re.shape[2]
    assert 2 * p == LANES and i * S5_CHUNK == 256

    def dup(a):
        return jnp.concatenate([a, a], axis=-1).reshape(g, 1, LANES).astype(F32)

    bt_re, bt_im = jnp.swapaxes(b_re, 1, 2).astype(F32), jnp.swapaxes(b_im, 1, 2).astype(F32)
    b1 = jnp.concatenate([bt_re, bt_im], axis=-1)
    bsw = jnp.concatenate([bt_im, bt_re], axis=-1)
    c1 = jnp.concatenate([c_re, c_im], axis=-1).astype(F32)
    csw = jnp.concatenate([c_im, c_re], axis=-1).astype(F32)
    ldt = jnp.broadcast_to(log_dt.astype(F32)[:, None, None], (g, 1, LANES))
    gb = 8 if g % 8 == 0 else 1

    def spec(rows, cols):
        return pl.BlockSpec((gb, rows, cols), lambda gi: (gi, 0, 0))

    return pl.pallas_call(
        functools.partial(_s5_param_kernel, reverse),
        grid=(g // gb,),
        in_specs=[spec(1, LANES)] * 3 + [spec(i, LANES)] * 4 + [spec(i, i)],
        out_specs=[spec(256, 2 * LANES), spec(256, LANES), spec(256, i), spec(S5_TAB_ROWS, LANES)],
        out_shape=[jax.ShapeDtypeStruct((g, 256, 2 * LANES), BF16),
                   jax.ShapeDtypeStruct((g, 256, LANES), BF16),
                   jax.ShapeDtypeStruct((g, 256, i), F32),
                   jax.ShapeDtypeStruct((g, S5_TAB_ROWS, LANES), F32)],
        compiler_params=_params("parallel"),
        name="s5_params",
    )(dup(lam_re), dup(lam_im), ldt, b1, bsw, c1, csw, k0)


def _toeplitz(kt_f, kt_b):
    t = S5_CHUNK
    g, rows, i = kt_f.shape
    z = jnp.concatenate([kt_f[:, :rows - i], kt_b], axis=1)
    cols = [z[:, (t - 1 - tt) * i:(t - 1 - tt) * i + rows] for tt in range(t)]
    return jnp.stack(cols, axis=2).reshape(g, rows, t * i)


def _slot_transpose8(vs, slot):
    vs = list(vs)
    for d in (4, 2, 1):
        keep = (slot & d) == 0
        new = list(vs)
        for r in range(8):
            if r & d == 0:
                a, b = vs[r], vs[r + d]
                new[r] = jnp.where(keep, a, pltpu.roll(b, 16 * d, 1))
                new[r + d] = jnp.where(keep, pltpu.roll(a, LANES - 16 * d, 1), b)
        vs = new
    return vs


def _gather_chunks(src_ref, dst_scr, n, rb):
    t = S5_CHUNK
    for r0 in range(0, n, rb):
        slot = lax.broadcasted_iota(jnp.int32, (rb, LANES), 1) // 16
        for hi in range(2):
            pieces = [src_ref[pl.ds(r0 * t + hi * 8 + r, rb, stride=t), :] for r in range(8)]
            outs = _slot_transpose8(pieces, slot)
            for gi in range(8):
                dst_scr[gi, r0:r0 + rb, hi * LANES:(hi + 1) * LANES] = outs[gi].astype(BF16)


def _s5_kernel(n, nc, rb, ux_ref, uc_ref, toep_ref, wf_ref, wb_ref, vf_ref, vb_ref, af_ref, ab_ref,
               y_ref, u_scr, ucx_scr, wf_scr, wfs_scr, wb_scr, wbs_scr, hf_scr, gb_scr):
    t = S5_CHUNK
    ntot = n + nc
    nblk = ntot // SUBLANES
    width = 8 * LANES
    _gather_chunks(ux_ref, u_scr, n, rb)
    _gather_chunks(uc_ref, ucx_scr, nc, min(rb, nc))

    for gi in range(8):
        ls = slice(gi * LANES, (gi + 1) * LANES)
        ug, ucg = u_scr[gi], ucx_scr[gi]
        for w_ref, x_rows, c_rows, dst, dst_s in ((wf_ref, slice(nc, ntot), slice(0, nc), wf_scr, wfs_scr),
                                                  (wb_ref, slice(0, n), slice(n, ntot), wb_scr, wbs_scr)):
            w_x = jnp.dot(ug, w_ref[gi], preferred_element_type=F32)
            w_c = jnp.dot(ucg, w_ref[gi], preferred_element_type=F32)
            dst[x_rows, ls], dst_s[x_rows, ls] = w_x[:, :LANES], w_x[:, LANES:]
            dst[c_rows, ls], dst_s[c_rows, ls] = w_c[:, :LANES], w_c[:, LANES:]

    def tables(a_ref):
        return [jnp.concatenate([a_ref[gi, k * SUBLANES:(k + 1) * SUBLANES, :] for gi in range(8)], axis=1)
                for k in range(S5_TAB_ROWS // SUBLANES)]

    tab_f, tab_b = tables(af_ref), tables(ab_ref)
    row = lax.broadcasted_iota(jnp.int32, (SUBLANES, width), 0)

    def block(src, src_s, dst, r0, h, hs, tab, reverse):
        x, xs = src[pl.ds(r0, SUBLANES), :], src_s[pl.ds(r0, SUBLANES), :]
        for k, d in enumerate((1, 2, 4)):
            sh = SUBLANES - d if reverse else d
            sx, sxs = pltpu.roll(x, sh, 0), pltpu.roll(xs, sh, 0)
            x, xs = x + tab[2 * k] * sx + tab[2 * k + 1] * sxs, xs + tab[2 * k] * sxs - tab[2 * k + 1] * sx
        before = pltpu.roll(x, SUBLANES - 1 if reverse else 1, 0)
        before = jnp.where(row == (SUBLANES - 1 if reverse else 0), 0.0, before)
        hb, hsb = jnp.broadcast_to(h, x.shape), jnp.broadcast_to(hs, x.shape)
        dst[pl.ds(r0, SUBLANES), :] = before + tab[6] * hb + tab[7] * hsb
        e = 0 if reverse else SUBLANES - 1
        q1, q2 = tab[8][0:1], tab[9][0:1]
        return x[e:e + 1] + q1 * h + q2 * hs, xs[e:e + 1] + q1 * hs - q2 * h

    def step(i, carry):
        h, hs, g, gs = carry
        h, hs = block(wf_scr, wfs_scr, hf_scr, pl.multiple_of(i * SUBLANES, SUBLANES), h, hs, tab_f, False)
        g, gs = block(wb_scr, wbs_scr, gb_scr, pl.multiple_of((nblk - 1 - i) * SUBLANES, SUBLANES), g, gs,
                      tab_b, True)
        return h, hs, g, gs

    zero = jnp.zeros((1, width), F32)
    lax.fori_loop(0, nblk, step, (zero, zero, zero, zero))

    nt = (((1,), (1,)), ((), ()))
    for r0 in range(0, n, rb):
        slot = lax.broadcasted_iota(jnp.int32, (rb, LANES), 1) // 16
        ys = []
        for gi in range(8):
            ls = slice(gi * LANES, (gi + 1) * LANES)
            y = jnp.dot(u_scr[gi, r0:r0 + rb, :], toep_ref[gi], preferred_element_type=F32)
            y += lax.dot_general(hf_scr[nc + r0:nc + r0 + rb, ls].astype(BF16), vf_ref[gi], nt,
                                 preferred_element_type=F32)
            y += lax.dot_general(gb_scr[r0:r0 + rb, ls].astype(BF16), vb_ref[gi], nt,
                                 preferred_element_type=F32)
            ys.append(y)
        for hi in range(2):
            outs = _slot_transpose8([ys[gi][:, hi * LANES:(hi + 1) * LANES] for gi in range(8)], slot)
            for r in range(8):
                y_ref[pl.ds(r0 * t + hi * 8 + r, rb, stride=t), :] = outs[r]


def _s5(u, uc, toep, wf2, wb2, vf, vb, af, ab):
    b, l, c = u.shape
    lc = uc.shape[1]
    t = S5_CHUNK
    n, nc = l // t, lc // t
    assert l % t == 0 and lc % t == 0 and c % LANES == 0 and n % SUBLANES == 0 and nc % SUBLANES == 0
    rb = _tile(n, 128, SUBLANES)
    ntot = n + nc
    gmat = lambda rows, last: pl.BlockSpec((8, rows, last), lambda bi, ci: (ci, 0, 0))
    return pl.pallas_call(
        functools.partial(_s5_kernel, n, nc, rb),
        grid=(b, c // LANES),
        in_specs=[pl.BlockSpec((None, l, LANES), lambda bi, ci: (bi, 0, ci)),
                  pl.BlockSpec((None, lc, LANES), lambda bi, ci: (bi, 0, ci)),
                  gmat(256, 256), gmat(256, 256), gmat(256, 256), gmat(256, LANES), gmat(256, LANES),
                  gmat(S5_TAB_ROWS, LANES), gmat(S5_TAB_ROWS, LANES)],
        out_specs=pl.BlockSpec((None, l, LANES), lambda bi, ci: (bi, 0, ci)),
        out_shape=jax.ShapeDtypeStruct((b, l, c), F32),
        scratch_shapes=[pltpu.VMEM((8, n, 256), BF16),
                        pltpu.VMEM((8, nc, 256), BF16)] + [pltpu.VMEM((ntot, 8 * LANES), F32)] * 6,
        compiler_params=_params("parallel", "parallel"),
        name="s5_scan",
    )(u, uc, toep, wf2, wb2, vf, vb, af, ab)


def _glu_kernel(nj, tn, y_ref, u_ref, d_ref, w_ref, b_ref, g_ref, o_ref, v32_scr, vb_scr):
    j = pl.program_id(1)

    @pl.when(j == 0)
    def _():
        v = jax.nn.gelu(y_ref[...] + d_ref[...] * u_ref[...])
        vb_scr[...] = v.astype(BF16)
        for jj in range(nj):
            v32_scr[jj] = v[:, jj * tn:(jj + 1) * tn]

    acc = jnp.dot(vb_scr[...], w_ref[...], preferred_element_type=F32) + b_ref[...]
    o_ref[...] = (v32_scr[j] * _sigmoid(acc) * g_ref[...].astype(F32)).astype(o_ref.dtype)


def _glu(y, u, d_skip, w_bf, b_glu, gate, tm):
    m, c = y.shape
    tn = _tile(c, 512, LANES)
    nj = c // tn
    return pl.pallas_call(
        functools.partial(_glu_kernel, nj, tn),
        grid=(m // tm, nj),
        in_specs=[pl.BlockSpec((tm, c), lambda i, j: (i, 0)),
                  pl.BlockSpec((tm, c), lambda i, j: (i, 0)),
                  pl.BlockSpec((1, c), lambda i, j: (0, 0)),
                  pl.BlockSpec((c, tn), lambda i, j: (0, j)),
                  pl.BlockSpec((1, tn), lambda i, j: (0, j)),
                  pl.BlockSpec((tm, tn), lambda i, j: (i, j))],
        out_specs=pl.BlockSpec((tm, tn), lambda i, j: (i, j)),
        out_shape=jax.ShapeDtypeStruct((m, c), BF16),
        scratch_shapes=[pltpu.VMEM((nj, tm, tn), F32), pltpu.VMEM((tm, c), BF16)],
        compiler_params=_params("parallel", "arbitrary"),
        name="ssm_glu",
    )(y, u, d_skip.reshape(1, c), w_bf, b_glu.reshape(1, c), gate)


def _outproj_kernel(a_ref, s_ref, wa_ref, ws_ref, x_ref, g_ref, o_ref):
    acc = jnp.dot(a_ref[...], wa_ref[...], preferred_element_type=F32)
    acc += jnp.dot(s_ref[...], ws_ref[...], preferred_element_type=F32)
    o_ref[...] = x_ref[...] + g_ref[...] * acc


def _outproj(lhs_a, lhs_s, col_s, w_bf, x2, gate, rows_per_batch, tm, name):
    m = lhs_a.shape[0]
    kh = w_bf.shape[0] // 2
    d = w_bf.shape[1]
    tn = _tile(d, 512, LANES)
    tpb = rows_per_batch // tm
    assert rows_per_batch % tm == 0 and lhs_a.shape[1] % kh == 0 and lhs_s.shape[1] % kh == 0
    return pl.pallas_call(
        _outproj_kernel,
        grid=(m // tm, d // tn),
        in_specs=[pl.BlockSpec((tm, kh), lambda i, j: (i, 0)),
                  pl.BlockSpec((tm, kh), lambda i, j: (i, col_s)),
                  pl.BlockSpec((kh, tn), lambda i, j: (0, j)),
                  pl.BlockSpec((kh, tn), lambda i, j: (1, j)),
                  pl.BlockSpec((tm, tn), lambda i, j: (i, j)),
                  pl.BlockSpec((None, 1, tn), lambda i, j: (i // tpb, 0, j))],
        out_specs=pl.BlockSpec((tm, tn), lambda i, j: (i, j)),
        out_shape=jax.ShapeDtypeStruct((m, d), F32),
        compiler_params=_params("parallel", "parallel"),
        name=name,
    )(lhs_a, lhs_s, w_bf, w_bf, x2, gate)


def _inproj1_kernel(tpb, x_ref, xp_ref, xq_ref, nw_ref, sh_ref, sc_ref, wb_ref, wc_ref, wx_ref, wg_ref,
                    cw_ref, cb_ref, o_ref, xn_scr):
    i = pl.program_id(0)
    tm = x_ref.shape[0]

    @pl.when(pl.program_id(1) == 0)
    def _():
        nw, sh, sc = nw_ref[...], sh_ref[...], sc_ref[...]
        xn_scr[0:HALO, :] = _modulated(xp_ref[...], nw, sh, sc).astype(BF16)
        xn_scr[HALO:HALO + tm, :] = _modulated(x_ref[...], nw, sh, sc).astype(BF16)
        xn_scr[HALO + tm:, :] = _modulated(xq_ref[...], nw, sh, sc).astype(BF16)

    xn_all = xn_scr[...]
    xn = xn_scr[HALO:HALO + tm, :]
    c_gate = jnp.dot(xn_all, wc_ref[...], preferred_element_type=F32)
    xin = jnp.dot(xn_all, wx_ref[...], preferred_element_type=F32)
    b_gate = jnp.dot(xn, wb_ref[...], preferred_element_type=F32)
    g = jnp.dot(xn, wg_ref[...], preferred_element_type=F32)

    y = c_gate * xin
    rows = y.shape[0]
    row = lax.broadcasted_iota(jnp.int32, y.shape, 0)
    lo = jnp.where((i % tpb) == 0, HALO, 0)
    hi = jnp.where((i % tpb) == tpb - 1, HALO + tm, rows)
    y = jnp.where((row >= lo) & (row < hi), y, 0.0)
    y_prev = pltpu.roll(y, 1, 0)[HALO:HALO + tm]
    y_next = pltpu.roll(y, rows - 1, 0)[HALO:HALO + tm]
    conv = cb_ref[...] + y_prev * cw_ref[0:1, :] + y[HALO:HALO + tm] * cw_ref[1:2, :] + y_next * cw_ref[2:3, :]
    o_ref[...] = (b_gate * conv * _silu(g)).astype(o_ref.dtype)


def _inproj1(x2, norm_w, shift, scale, w_bf, conv_w, conv_b, rows_per_batch, tm):
    m, d = x2.shape
    c = w_bf.shape[1] // 4
    tn = _tile(c, 256, LANES)
    nj = c // tn
    tpb = rows_per_batch // tm
    hb = tm // HALO
    nhb = m // HALO
    assert rows_per_batch % tm == 0 and tm % HALO == 0

    def wspec(p):
        return pl.BlockSpec((d, tn), lambda i, j: (0, p * nj + j))

    return pl.pallas_call(
        functools.partial(_inproj1_kernel, tpb),
        grid=(m // tm, nj),
        in_specs=[pl.BlockSpec((tm, d), lambda i, j: (i, 0)),
                  pl.BlockSpec((HALO, d), lambda i, j: (jnp.maximum(i * hb - 1, 0), 0)),
                  pl.BlockSpec((HALO, d), lambda i, j: (jnp.minimum((i + 1) * hb, nhb - 1), 0)),
                  pl.BlockSpec((1, d), lambda i, j: (0, 0)),
                  pl.BlockSpec((None, 1, d), lambda i, j: (i // tpb, 0, 0)),
                  pl.BlockSpec((None, 1, d), lambda i, j: (i // tpb, 0, 0)),
                  wspec(0), wspec(1), wspec(2), wspec(3),
                  pl.BlockSpec((CONV_K, tn), lambda i, j: (0, j)),
                  pl.BlockSpec((1, tn), lambda i, j: (0, j))],
        out_specs=pl.BlockSpec((tm, tn), lambda i, j: (i, j)),
        out_shape=jax.ShapeDtypeStruct((m, c), BF16),
        scratch_shapes=[pltpu.VMEM((tm + 2 * HALO, d), BF16)],
        compiler_params=_params("parallel", "arbitrary"),
        name="inproj1",
    )(x2, x2, x2, norm_w.reshape(1, d), shift, scale, w_bf, w_bf, w_bf, w_bf, conv_w, conv_b.reshape(1, c))


def _rope_tables(n_tok):
    rows = n_tok // GRID_W
    r, col = jnp.meshgrid(jnp.arange(rows, dtype=F32), jnp.arange(GRID_W, dtype=F32), indexing="ij")
    axis_dim = HEAD_DIM // 2
    inv_freq = ROPE_THETA ** (-jnp.arange(0, axis_dim, 2, dtype=F32) / axis_dim)
    ar = r.reshape(-1)[:, None] * inv_freq
    ac = col.reshape(-1)[:, None] * inv_freq
    cos_t = jnp.concatenate([jnp.cos(ar), jnp.cos(ar), jnp.cos(ac), jnp.cos(ac)], axis=1)
    sin_t = jnp.concatenate([-jnp.sin(ar), jnp.sin(ar), -jnp.sin(ac), jnp.sin(ac)], axis=1)
    return cos_t, sin_t


def kernel(x, c, ctx, c_ctx, l0_norm_w, l0_w_mod, l0_b_mod, l0_w_in, l0_q_norm_w, l0_k_norm_w, l0_fwd_lam_re, l0_fwd_lam_im, l0_fwd_log_dt, l0_fwd_b_re, l0_fwd_b_im, l0_fwd_c_re, l0_fwd_c_im, l0_bwd_lam_re, l0_bwd_lam_im, l0_bwd_log_dt, l0_bwd_b_re, l0_bwd_b_im, l0_bwd_c_re, l0_bwd_c_im, l0_ssm_d, l0_w_glu, l0_b_glu, l0_w_out, l1_norm_w, l1_w_mod, l1_b_mod, l1_w_in, l1_conv_w, l1_conv_b, l1_w_out):
    b, l, d = x.shape
    lc = ctx.shape[1]
    ssm_w = l0_ssm_d.shape[0]
    att_w = l0_w_out.shape[0] - ssm_w
    kv_w = att_w // Q_PER_KV
    widths = (att_w, kv_w, kv_w, att_w, ssm_w, ssm_w)
    assert sum(widths) == l0_w_in.shape[1]

    pad = (-(b + 1)) % 16
    cond = jnp.concatenate([c, c_ctx[None, :], jnp.zeros((pad, d), F32)], axis=0)
    mod0 = _ada_mod(cond, l0_w_mod, l0_b_mod)
    mod1 = _ada_mod(cond, l1_w_mod, l1_b_mod)
    shift0, scale0, gate0 = (mod0[:b, k * d:(k + 1) * d].reshape(b, 1, d) for k in range(3))
    shift0c, scale0c = (jnp.broadcast_to(mod0[b, k * d:(k + 1) * d].reshape(1, 1, d), (b, 1, d)) for k in range(2))
    shift1, scale1, gate1 = (mod1[:b, k * d:(k + 1) * d].reshape(b, 1, d) for k in range(3))

    x2 = x.reshape(b * l, d)
    ctx2 = ctx.reshape(b * lc, d)
    w_in0 = l0_w_in.astype(BF16)

    cos_t, sin_t = _rope_tables(l)
    tm = _tile(l, 512, SUBLANES)
    q, k, v, g_att, u, g_ssm = _inproj0(x2, l0_norm_w, shift0, scale0, w_in0, cos_t, sin_t,
                                        l0_q_norm_w, l0_k_norm_w, widths, l, tm)
    tmc = _tile(lc, 256, SUBLANES)
    ones_t, zeros_t = jnp.ones((lc, HEAD_DIM), F32), jnp.zeros((lc, HEAD_DIM), F32)
    _, kc, vc, _, uc, _ = _inproj0(ctx2, l0_norm_w, shift0c, scale0c, w_in0, ones_t, zeros_t,
                                   l0_q_norm_w, l0_k_norm_w, widths, lc, tmc)

    k_all = jnp.concatenate([kc.reshape(b, lc, kv_w), k.reshape(b, l, kv_w)], axis=1)
    v_all = jnp.concatenate([vc.reshape(b, lc, kv_w), v.reshape(b, l, kv_w)], axis=1)
    score_bound = HEAD_DIM ** 0.5 * jnp.max(jnp.abs(l0_q_norm_w)) * jnp.max(jnp.abs(l0_k_norm_w))
    att = lax.cond(score_bound < SCORE_BOUND_NO_SHIFT,
                   functools.partial(_attention_noshift, tq=_tile(l, 256, SUBLANES)),
                   functools.partial(_attention, tq=_tile(l, 512, SUBLANES)),
                   q.reshape(b, l, att_w), k_all, v_all, g_att.reshape(b, l, att_w))

    fwd = (l0_fwd_lam_re, l0_fwd_lam_im, l0_fwd_log_dt, l0_fwd_b_re, l0_fwd_b_im, l0_fwd_c_re, l0_fwd_c_im)
    bwd = (l0_bwd_lam_re, l0_bwd_lam_im, l0_bwd_log_dt, l0_bwd_b_re, l0_bwd_b_im, l0_bwd_c_re, l0_bwd_c_im)
    n_in = l0_fwd_b_re.shape[2]
    wf, vf, ktf, af = _s5_params(fwd, False, jnp.zeros((ssm_w // n_in, n_in, n_in), F32))
    wb, vb, ktb, ab = _s5_params(bwd, True, ktf[:, -n_in:, :])
    toep = _toeplitz(ktf, ktb).astype(BF16)
    y = _s5(u.reshape(b, l, ssm_w), uc.reshape(b, lc, ssm_w), toep, wf, wb, vf, vb, af, ab)
    ssm = _glu(y.reshape(b * l, ssm_w), u, l0_ssm_d, l0_w_glu.astype(BF16), l0_b_glu, g_ssm, tm)

    tmo = _tile(l, 1024, SUBLANES)
    assert att_w == ssm_w
    h1 = _outproj(att.reshape(b * l, att_w), ssm, 0, l0_w_out.astype(BF16), x2, gate0, l, tmo, "outproj0")

    mix1 = _inproj1(h1, l1_norm_w, shift1, scale1, l1_w_in.astype(BF16), l1_conv_w, l1_conv_b, l, tm)
    h2 = _outproj(mix1, mix1, 1, l1_w_out.astype(BF16), h1, gate1, l, tmo, "outproj1")
    return h2.reshape(b, l, d)
```

```python
import functools
import math

import jax
import jax.numpy as jnp
from jax import lax
from jax.experimental import pallas as pl
from jax.experimental.pallas import tpu as pltpu

F32 = jnp.float32
BF16 = jnp.bfloat16

EPS = 1e-6
HEAD_DIM = 128
Q_PER_KV = 4
GRID_W = 64
ROPE_THETA = 10000.0
CONV_K = 3
Q_SCALE = HEAD_DIM ** -0.5 * math.log2(math.e)
SCORE_BOUND_NO_SHIFT = 40.0

LANES = 128
SUBLANES = 8
HALO = 16
CAST_ROWS = 16
S5_CHUNK = 16
S5_POW_ROWS = 32
S5_TAB_ROWS = 10 * SUBLANES
VMEM_LIMIT_BYTES = 56 * 1024 * 1024


def _params(*semantics):
    return pltpu.CompilerParams(dimension_semantics=semantics, vmem_limit_bytes=VMEM_LIMIT_BYTES)


def _tile(dim, pref, mult):
    if dim <= pref:
        return dim
    t = (pref // mult) * mult
    while t > mult and dim % t:
        t -= mult
    assert dim % t == 0, (dim, pref, mult)
    return t


def _sigmoid(x):
    return 1.0 / (1.0 + jnp.exp(-x))


def _silu(x):
    return x * _sigmoid(x)


def _ada_kernel(c_ref, w_ref, b_ref, o_ref):
    s = _silu(c_ref[...]).astype(BF16)
    o_ref[...] = jnp.dot(s, w_ref[...].astype(BF16), preferred_element_type=F32) + b_ref[...]


def _ada_mod(cond, w_mod, b_mod):
    rows, d = cond.shape
    n = w_mod.shape[1]
    tn = _tile(n, 512, LANES)
    return pl.pallas_call(
        _ada_kernel,
        grid=(n // tn,),
        in_specs=[pl.BlockSpec((rows, d), lambda j: (0, 0)),
                  pl.BlockSpec((d, tn), lambda j: (0, j)),
                  pl.BlockSpec((1, tn), lambda j: (0, j))],
        out_specs=pl.BlockSpec((rows, tn), lambda j: (0, j)),
        out_shape=jax.ShapeDtypeStruct((rows, n), F32),
        compiler_params=_params("parallel"),
        name="ada_mod",
    )(cond, w_mod, b_mod.reshape(1, n))


def _modulated(x, nw, shift, scale):
    ms = jnp.mean(x * x, axis=-1, keepdims=True)
    return (x * lax.rsqrt(ms + EPS) * nw) * (1.0 + scale) + shift


def _norm_rope_heads(acc, nw, cos, sin, out_scale, o_ref):
    lane = lax.broadcasted_iota(jnp.int32, (acc.shape[0], HEAD_DIM), 1)
    first_half = (lane % 64) < 32
    for h in range(acc.shape[1] // HEAD_DIM):
        a = acc[:, h * HEAD_DIM:(h + 1) * HEAD_DIM]
        ms = jnp.mean(a * a, axis=-1, keepdims=True)
        a = a * lax.rsqrt(ms + EPS) * nw
        swapped = jnp.where(first_half, pltpu.roll(a, HEAD_DIM - 32, 1), pltpu.roll(a, 32, 1))
        o_ref[:, h * HEAD_DIM:(h + 1) * HEAD_DIM] = ((a * cos + swapped * sin) * out_scale).astype(o_ref.dtype)


def _inproj0_kernel(bounds, n_cast, x_ref, nw_ref, sh_ref, sc_ref, w_ref, cos_ref, sin_ref, qn_ref, kn_ref, *rest):
    cast_in = rest[:n_cast]
    q_ref, k_ref, v_ref, ga_ref, u_ref, gs_ref = rest[n_cast:n_cast + 6]
    cast_out = rest[n_cast + 6:2 * n_cast + 6]
    xn_scr, acc_scr = rest[2 * n_cast + 6:]
    j = pl.program_id(1)
    e_q, e_k, e_v, e_ga, e_u = bounds

    def dot():
        for src, dst in zip(cast_in, cast_out):
            dst[...] = src[...].astype(BF16)
        return jnp.dot(xn_scr[...], w_ref[...], preferred_element_type=F32)

    @pl.when(j == 0)
    def _():
        xn_scr[...] = _modulated(x_ref[...], nw_ref[...], sh_ref[...], sc_ref[...]).astype(BF16)
        acc_scr[0] = dot()

    @pl.when((j >= 1) & (j <= e_q))
    def _():
        _norm_rope_heads(acc_scr[(j - 1) % 2], qn_ref[...], cos_ref[...], sin_ref[...], Q_SCALE, q_ref)
        acc_scr[j % 2] = dot()

    @pl.when(j == e_k)
    def _():
        _norm_rope_heads(acc_scr[(j - 1) % 2], kn_ref[...], cos_ref[...], sin_ref[...], 1.0, k_ref)
        v_ref[...] = dot().astype(BF16)

    @pl.when((j >= e_v) & (j < e_ga))
    def _():
        ga_ref[...] = _silu(dot()).astype(BF16)

    @pl.when((j >= e_ga) & (j < e_u))
    def _():
        u_ref[...] = dot()

    @pl.when(j >= e_u)
    def _():
        gs_ref[...] = _silu(dot()).astype(BF16)


def _inproj0(x2, norm_w, shift, scale, w_bf, cos_t, sin_t, qn, kn, widths, rows_per_batch, tm, side_casts=()):
    m, d = x2.shape
    n = w_bf.shape[1]
    tn = _tile(min(widths), 512, LANES)
    assert all(w % tn == 0 for w in widths) and sum(widths) == n
    ends, off = [], 0
    for w in widths:
        off += w
        ends.append(off // tn)
    starts = [0] + ends[:-1]
    assert ends[1] - starts[1] == 1 and ends[2] - starts[2] == 1, "k and v must be one column tile each"
    tpb = rows_per_batch // tm
    assert rows_per_batch % tm == 0 and m % tm == 0

    def out_spec(p):
        lo, hi = starts[p], ends[p]
        late = 1 if p == 0 else 0
        return pl.BlockSpec((tm, tn), lambda i, j: (i, jnp.clip(j - late - lo, 0, hi - lo - 1)))

    nj = n // tn
    steps = (m // tm) * nj
    cast_specs = []
    for a in side_casts:
        slab = CAST_ROWS * -(-a.shape[0] // (CAST_ROWS * steps))
        nblk = a.shape[0] // slab
        assert a.shape[0] % slab == 0 and nblk <= steps, a.shape
        cast_specs.append(pl.BlockSpec((slab, a.shape[1]),
                                       lambda i, j, nblk=nblk: (jnp.minimum(i * nj + j, nblk - 1), 0)))

    dts = (BF16, BF16, BF16, BF16, F32, BF16)
    return pl.pallas_call(
        functools.partial(_inproj0_kernel, tuple(ends[:5]), len(side_casts)),
        grid=(m // tm, n // tn),
        in_specs=[pl.BlockSpec((tm, d), lambda i, j: (i, 0)),
                  pl.BlockSpec((1, d), lambda i, j: (0, 0)),
                  pl.BlockSpec((None, 1, d), lambda i, j: (i // tpb, 0, 0)),
                  pl.BlockSpec((None, 1, d), lambda i, j: (i // tpb, 0, 0)),
                  pl.BlockSpec((d, tn), lambda i, j: (0, j)),
                  pl.BlockSpec((tm, HEAD_DIM), lambda i, j: (i % tpb, 0)),
                  pl.BlockSpec((tm, HEAD_DIM), lambda i, j: (i % tpb, 0)),
                  pl.BlockSpec((1, HEAD_DIM), lambda i, j: (0, 0)),
                  pl.BlockSpec((1, HEAD_DIM), lambda i, j: (0, 0))] + cast_specs,
        out_specs=[out_spec(p) for p in range(6)] + cast_specs,
        out_shape=[jax.ShapeDtypeStruct((m, w), dt) for w, dt in zip(widths, dts)]
        + [jax.ShapeDtypeStruct(a.shape, BF16) for a in side_casts],
        scratch_shapes=[pltpu.VMEM((tm, d), BF16), pltpu.VMEM((2, tm, tn), F32)],
        compiler_params=_params("arbitrary", "arbitrary"),
        name="inproj0",
    )(x2, norm_w.reshape(1, d), shift, scale, w_bf, cos_t, sin_t, qn.reshape(1, HEAD_DIM), kn.reshape(1, HEAD_DIM),
      *side_casts)


def _attn_kernel(q_ref, k_ref, v_ref, g_ref, o_ref, m_scr, l_scr, acc_scr):
    kv = pl.program_id(3)

    @pl.when(kv == 0)
    def _():
        m_scr[...] = jnp.full(m_scr.shape, -jnp.inf, F32)
        l_scr[...] = jnp.zeros(l_scr.shape, F32)
        acc_scr[...] = jnp.zeros(acc_scr.shape, F32)

    k = k_ref[...]
    v = v_ref[...]
    for h in range(Q_PER_KV):
        qh = q_ref[:, h * HEAD_DIM:(h + 1) * HEAD_DIM]
        s = lax.dot_general(qh, k, (((1,), (1,)), ((), ())), preferred_element_type=F32)
        m_prev = m_scr[h]
        m_new = jnp.maximum(m_prev, jnp.max(s, axis=1, keepdims=True))
        alpha = jnp.exp2(m_prev - m_new)
        p = jnp.exp2(s - m_new[:, :1])
        l_scr[h] = alpha * l_scr[h] + jnp.sum(p, axis=1, keepdims=True)
        acc_scr[h] = alpha * acc_scr[h] + jnp.dot(p.astype(BF16), v, preferred_element_type=F32)
        m_scr[h] = m_new

    @pl.when(kv == pl.num_programs(3) - 1)
    def _():
        for h in range(Q_PER_KV):
            sl = slice(h * HEAD_DIM, (h + 1) * HEAD_DIM)
            o = acc_scr[h] / l_scr[h]
            o_ref[:, sl] = (o * g_ref[:, sl].astype(F32)).astype(o_ref.dtype)


def _attn_noshift_kernel(q_ref, k_ref, v_ref, g_ref, o_ref):
    tq = q_ref.shape[0]
    q = jnp.concatenate([q_ref[:, h * HEAD_DIM:(h + 1) * HEAD_DIM] for h in range(Q_PER_KV)], axis=0)
    s = lax.dot_general(q, k_ref[...], (((1,), (1,)), ((), ())), preferred_element_type=F32)
    p = jnp.exp2(s).astype(BF16)
    v = v_ref[...]
    v_ext = jnp.concatenate([v, jnp.ones_like(v)], axis=1)
    a = jnp.dot(p, v_ext, preferred_element_type=F32)
    for h in range(Q_PER_KV):
        sl = slice(h * HEAD_DIM, (h + 1) * HEAD_DIM)
        ah = a[h * tq:(h + 1) * tq]
        o = ah[:, :HEAD_DIM] / ah[:, HEAD_DIM:]
        o_ref[:, sl] = (o * g_ref[:, sl].astype(F32)).astype(o_ref.dtype)


def _attention_noshift(q, k_all, v_all, gate, tq):
    b, l, aw = q.shape
    s_len, kvw = k_all.shape[1], k_all.shape[2]
    kvh = kvw // HEAD_DIM
    gw = Q_PER_KV * HEAD_DIM
    assert aw == kvh * gw
    return pl.pallas_call(
        _attn_noshift_kernel,
        grid=(b, kvh, l // tq),
        in_specs=[pl.BlockSpec((None, tq, gw), lambda bi, h, qi: (bi, qi, h)),
                  pl.BlockSpec((None, s_len, HEAD_DIM), lambda bi, h, qi: (bi, 0, h)),
                  pl.BlockSpec((None, s_len, HEAD_DIM), lambda bi, h, qi: (bi, 0, h)),
                  pl.BlockSpec((None, tq, gw), lambda bi, h, qi: (bi, qi, h))],
        out_specs=pl.BlockSpec((None, tq, gw), lambda bi, h, qi: (bi, qi, h)),
        out_shape=jax.ShapeDtypeStruct((b, l, aw), BF16),
        compiler_params=_params("parallel", "parallel", "parallel"),
        name="attention_noshift",
    )(q, k_all, v_all, gate)


def _attention(q, k_all, v_all, gate, tq):
    tk = _tile(k_all.shape[1], 1408, LANES)
    b, l, aw = q.shape
    s_len, kvw = k_all.shape[1], k_all.shape[2]
    kvh = kvw // HEAD_DIM
    gw = Q_PER_KV * HEAD_DIM
    assert aw == kvh * gw
    return pl.pallas_call(
        _attn_kernel,
        grid=(b, kvh, l // tq, s_len // tk),
        in_specs=[pl.BlockSpec((None, tq, gw), lambda bi, h, qi, ki: (bi, qi, h)),
                  pl.BlockSpec((None, tk, HEAD_DIM), lambda bi, h, qi, ki: (bi, ki, h)),
                  pl.BlockSpec((None, tk, HEAD_DIM), lambda bi, h, qi, ki: (bi, ki, h)),
                  pl.BlockSpec((None, tq, gw), lambda bi, h, qi, ki: (bi, qi, h))],
        out_specs=pl.BlockSpec((None, tq, gw), lambda bi, h, qi, ki: (bi, qi, h)),
        out_shape=jax.ShapeDtypeStruct((b, l, aw), BF16),
        scratch_shapes=[pltpu.VMEM((Q_PER_KV, tq, HEAD_DIM), F32)] * 3,
        compiler_params=_params("parallel", "parallel", "parallel", "arbitrary"),
        name="attention",
    )(q, k_all, v_all, gate)


def _s5_param_kernel(reverse, lr_ref, li_ref, ldt_ref, b1_ref, bsw_ref, c1_ref, csw_ref, k0_ref,
                     w_ref, vt_ref, kt_ref, a_ref):
    for gi in range(lr_ref.shape[0]):
        _s5_group_params(reverse, lr_ref.at[gi], li_ref.at[gi], ldt_ref.at[gi], b1_ref.at[gi], bsw_ref.at[gi],
                         c1_ref.at[gi], csw_ref.at[gi], k0_ref.at[gi],
                         w_ref.at[gi], vt_ref.at[gi], kt_ref.at[gi], a_ref.at[gi])


def _s5_group_params(reverse, lr_ref, li_ref, ldt_ref, b1_ref, bsw_ref, c1_ref, csw_ref, k0_ref,
                     w_ref, vt_ref, kt_ref, a_ref):
    t = S5_CHUNK
    lr, li = lr_ref[...], li_ref[...]
    dt = jnp.exp(ldt_ref[...])
    lane = lax.broadcasted_iota(jnp.int32, (1, LANES), 1)
    neg_first = jnp.where(lane < 64, -1.0, 1.0).astype(F32)
    jj = lax.broadcasted_iota(jnp.int32, (S5_POW_ROWS, LANES), 0).astype(F32)
    mag = jnp.exp(lr * dt * jj)
    ang = li * dt * jj
    p_re = mag * jnp.cos(ang)
    p_im = mag * jnp.sin(ang)

    a_re, a_im = p_re[1:2], p_im[1:2]
    den = lr * lr + li * li
    n_re = a_re - 1.0
    coef_re = (n_re * lr + a_im * li) / den
    coef_im = (a_im * lr - n_re * li) / den
    b1 = b1_ref[...]
    b2 = bsw_ref[...] * neg_first
    bb1 = coef_re * b1 + coef_im * b2
    bb2 = coef_re * b2 - coef_im * b1

    c1 = c1_ref[...] * (-neg_first)
    csw = csw_ref[...]

    xs = [None] * t
    for j in range(t):
        x_j = p_re[j:j + 1] * bb1 + p_im[j:j + 1] * bb2
        blk = (t - 1 - j) if not reverse else j
        xs[blk] = x_j
        w_ref[blk * 16:(blk + 1) * 16, 0:LANES] = x_j.astype(w_ref.dtype)
        w_ref[blk * 16:(blk + 1) * 16, LANES:2 * LANES] = pltpu.roll(x_j, LANES // 2, 1).astype(w_ref.dtype)
        m = (j + 1) if not reverse else (t - j)
        vt_ref[j * 16:(j + 1) * 16, :] = (p_re[m:m + 1] * c1 - p_im[m:m + 1] * csw).astype(vt_ref.dtype)
    x_all = jnp.concatenate(xs, axis=0)
    kt = lax.dot_general(x_all, c1, (((1,), (1,)), ((), ())),
                         precision=lax.Precision.HIGHEST, preferred_element_type=F32)
    kt_ref[...] = kt
    lag0 = 0 if reverse else (t - 1) * 16
    kt_ref[lag0:lag0 + 16, :] = kt[lag0:lag0 + 16] + k0_ref[...]

    r = lax.broadcasted_iota(jnp.int32, (SUBLANES, LANES), 0)
    last = SUBLANES - 1

    def table(k, expo, keep):
        e = expo.astype(F32) * float(t)
        mg = jnp.exp(lr * dt * e)
        an = li * dt * e
        a_ref[2 * k * SUBLANES:(2 * k + 1) * SUBLANES, :] = jnp.where(keep, mg * jnp.cos(an), 0.0)
        a_ref[(2 * k + 1) * SUBLANES:(2 * k + 2) * SUBLANES, :] = jnp.where(keep, mg * jnp.sin(an) * neg_first, 0.0)

    for k, d in enumerate((1, 2, 4)):
        table(k, jnp.full_like(r, d), (r <= last - d) if reverse else (r >= d))
    table(3, (last - r) if reverse else r, r >= 0)
    table(4, jnp.full_like(r, SUBLANES), r >= 0)


def _s5_params(prm, reverse, k0):
    lam_re, lam_im, log_dt, b_re, b_im, c_re, c_im = prm
    g, p = lam_re.shape
    i = b_re.shape[2]
    assert 2 * p == LANES and i * S5_CHUNK == 256

    def dup(a):
        return jnp.concatenate([a, a], axis=-1).reshape(g, 1, LANES).astype(F32)

    bt_re, bt_im = jnp.swapaxes(b_re, 1, 2).astype(F32), jnp.swapaxes(b_im, 1, 2).astype(F32)
    b1 = jnp.concatenate([bt_re, bt_im], axis=-1)
    bsw = jnp.concatenate([bt_im, bt_re], axis=-1)
    c1 = jnp.concatenate([c_re, c_im], axis=-1).astype(F32)
    csw = jnp.concatenate([c_im, c_re], axis=-1).astype(F32)
    ldt = jnp.broadcast_to(log_dt.astype(F32)[:, None, None], (g, 1, LANES))
    gb = 8 if g % 8 == 0 else 1

    def spec(rows, cols):
        return pl.BlockSpec((gb, rows, cols), lambda gi: (gi, 0, 0))

    return pl.pallas_call(
        functools.partial(_s5_param_kernel, reverse),
        grid=(g // gb,),
        in_specs=[spec(1, LANES)] * 3 + [spec(i, LANES)] * 4 + [spec(i, i)],
        out_specs=[spec(256, 2 * LANES), spec(256, LANES), spec(256, i), spec(S5_TAB_ROWS, LANES)],
        out_shape=[jax.ShapeDtypeStruct((g, 256, 2 * LANES), BF16),
                   jax.ShapeDtypeStruct((g, 256, LANES), BF16),
                   jax.ShapeDtypeStruct((g, 256, i), F32),
                   jax.ShapeDtypeStruct((g, S5_TAB_ROWS, LANES), F32)],
        compiler_params=_params("parallel"),
        name="s5_params",
    )(dup(lam_re), dup(lam_im), ldt, b1, bsw, c1, csw, k0)


def _toeplitz(kt_f, kt_b):
    t = S5_CHUNK
    g, rows, i = kt_f.shape
    z = jnp.concatenate([kt_f[:, :rows - i], kt_b], axis=1)
    cols = [z[:, (t - 1 - tt) * i:(t - 1 - tt) * i + rows] for tt in range(t)]
    return jnp.stack(cols, axis=2).reshape(g, rows, t * i)


def _slot_transpose8(vs, slot):
    vs = list(vs)
    for d in (4, 2, 1):
        keep = (slot & d) == 0
        new = list(vs)
        for r in range(8):
            if r & d == 0:
                a, b = vs[r], vs[r + d]
                new[r] = jnp.where(keep, a, pltpu.roll(b, 16 * d, 1))
                new[r + d] = jnp.where(keep, pltpu.roll(a, LANES - 16 * d, 1), b)
        vs = new
    return vs


def _gather_chunks(src_ref, dst_scr, n, rb):
    t = S5_CHUNK
    for r0 in range(0, n, rb):
        slot = lax.broadcasted_iota(jnp.int32, (rb, LANES), 1) // 16
        for hi in range(2):
            pieces = [src_ref[pl.ds(r0 * t + hi * 8 + r, rb, stride=t), :] for r in range(8)]
            outs = _slot_transpose8(pieces, slot)
            for gi in range(8):
                dst_scr[gi, r0:r0 + rb, hi * LANES:(hi + 1) * LANES] = outs[gi].astype(BF16)


def _s5_kernel(n, nc, rb, ux_ref, uc_ref, d_ref, toep_ref, wf_ref, wb_ref, vf_ref, vb_ref, af_ref, ab_ref,
               y_ref, u_scr, ucx_scr, wf_scr, wfs_scr, wb_scr, wbs_scr, hf_scr, gb_scr):
    t = S5_CHUNK
    ntot = n + nc
    nblk = ntot // SUBLANES
    width = 8 * LANES
    _gather_chunks(ux_ref, u_scr, n, rb)
    _gather_chunks(uc_ref, ucx_scr, nc, min(rb, nc))

    for gi in range(8):
        ls = slice(gi * LANES, (gi + 1) * LANES)
        ug, ucg = u_scr[gi], ucx_scr[gi]
        for w_ref, x_rows, c_rows, dst, dst_s in ((wf_ref, slice(nc, ntot), slice(0, nc), wf_scr, wfs_scr),
                                                  (wb_ref, slice(0, n), slice(n, ntot), wb_scr, wbs_scr)):
            w_x = jnp.dot(ug, w_ref[gi], preferred_element_type=F32)
            w_c = jnp.dot(ucg, w_ref[gi], preferred_element_type=F32)
            dst[x_rows, ls], dst_s[x_rows, ls] = w_x[:, :LANES], w_x[:, LANES:]
            dst[c_rows, ls], dst_s[c_rows, ls] = w_c[:, :LANES], w_c[:, LANES:]

    def tables(a_ref):
        return [jnp.concatenate([a_ref[gi, k * SUBLANES:(k + 1) * SUBLANES, :] for gi in range(8)], axis=1)
                for k in range(S5_TAB_ROWS // SUBLANES)]

    tab_f, tab_b = tables(af_ref), tables(ab_ref)
    row = lax.broadcasted_iota(jnp.int32, (SUBLANES, width), 0)

    def block(src, src_s, dst, r0, h, hs, tab, reverse):
        x, xs = src[pl.ds(r0, SUBLANES), :], src_s[pl.ds(r0, SUBLANES), :]
        for k, d in enumerate((1, 2, 4)):
            sh = SUBLANES - d if reverse else d
            sx, sxs = pltpu.roll(x, sh, 0), pltpu.roll(xs, sh, 0)
            x, xs = x + tab[2 * k] * sx + tab[2 * k + 1] * sxs, xs + tab[2 * k] * sxs - tab[2 * k + 1] * sx
        before = pltpu.roll(x, SUBLANES - 1 if reverse else 1, 0)
        before = jnp.where(row == (SUBLANES - 1 if reverse else 0), 0.0, before)
        hb, hsb = jnp.broadcast_to(h, x.shape), jnp.broadcast_to(hs, x.shape)
        dst[pl.ds(r0, SUBLANES), :] = before + tab[6] * hb + tab[7] * hsb
        e = 0 if reverse else SUBLANES - 1
        q1, q2 = tab[8][0:1], tab[9][0:1]
        return x[e:e + 1] + q1 * h + q2 * hs, xs[e:e + 1] + q1 * hs - q2 * h

    def step(i, carry):
        h, hs, g, gs = carry
        h, hs = block(wf_scr, wfs_scr, hf_scr, pl.multiple_of(i * SUBLANES, SUBLANES), h, hs, tab_f, False)
        g, gs = block(wb_scr, wbs_scr, gb_scr, pl.multiple_of((nblk - 1 - i) * SUBLANES, SUBLANES), g, gs,
                      tab_b, True)
        return h, hs, g, gs

    zero = jnp.zeros((1, width), F32)
    lax.fori_loop(0, nblk, step, (zero, zero, zero, zero))

    nt = (((1,), (1,)), ((), ()))
    for r0 in range(0, n, rb):
        slot = lax.broadcasted_iota(jnp.int32, (rb, LANES), 1) // 16
        ys = []
        for gi in range(8):
            ls = slice(gi * LANES, (gi + 1) * LANES)
            y = jnp.dot(u_scr[gi, r0:r0 + rb, :], toep_ref[gi], preferred_element_type=F32)
            y += lax.dot_general(hf_scr[nc + r0:nc + r0 + rb, ls].astype(BF16), vf_ref[gi], nt,
                                 preferred_element_type=F32)
            y += lax.dot_general(gb_scr[r0:r0 + rb, ls].astype(BF16), vb_ref[gi], nt,
                                 preferred_element_type=F32)
            ys.append(y)
        for hi in range(2):
            outs = _slot_transpose8([ys[gi][:, hi * LANES:(hi + 1) * LANES] for gi in range(8)], slot)
            for r in range(8):
                rows = pl.ds(r0 * t + hi * 8 + r, rb, stride=t)
                y_ref[rows, :] = jax.nn.gelu(outs[r] + d_ref[...] * ux_ref[rows, :])


def _s5(u, uc, d_skip, toep, wf2, wb2, vf, vb, af, ab):
    b, l, c = u.shape
    lc = uc.shape[1]
    t = S5_CHUNK
    n, nc = l // t, lc // t
    assert l % t == 0 and lc % t == 0 and c % LANES == 0 and n % SUBLANES == 0 and nc % SUBLANES == 0
    rb = _tile(n, 128, SUBLANES)
    ntot = n + nc
    gmat = lambda rows, last: pl.BlockSpec((8, rows, last), lambda bi, ci: (ci, 0, 0))
    return pl.pallas_call(
        functools.partial(_s5_kernel, n, nc, rb),
        grid=(b, c // LANES),
        in_specs=[pl.BlockSpec((None, l, LANES), lambda bi, ci: (bi, 0, ci)),
                  pl.BlockSpec((None, lc, LANES), lambda bi, ci: (bi, 0, ci)),
                  pl.BlockSpec((1, LANES), lambda bi, ci: (0, ci)),
                  gmat(256, 256), gmat(256, 256), gmat(256, 256), gmat(256, LANES), gmat(256, LANES),
                  gmat(S5_TAB_ROWS, LANES), gmat(S5_TAB_ROWS, LANES)],
        out_specs=pl.BlockSpec((None, l, LANES), lambda bi, ci: (bi, 0, ci)),
        out_shape=jax.ShapeDtypeStruct((b, l, c), F32),
        scratch_shapes=[pltpu.VMEM((8, n, 256), BF16),
                        pltpu.VMEM((8, nc, 256), BF16)] + [pltpu.VMEM((ntot, 8 * LANES), F32)] * 6,
        compiler_params=_params("parallel", "parallel"),
        name="s5_scan",
    )(u, uc, d_skip.reshape(1, c), toep, wf2, wb2, vf, vb, af, ab)


def _glu_kernel(v_ref, vj_ref, w_ref, b_ref, g_ref, o_ref, vb_scr):
    @pl.when(pl.program_id(1) == 0)
    def _():
        vb_scr[...] = v_ref[...].astype(BF16)

    acc = jnp.dot(vb_scr[...], w_ref[...], preferred_element_type=F32) + b_ref[...]
    o_ref[...] = (vj_ref[...] * _sigmoid(acc) * g_ref[...].astype(F32)).astype(o_ref.dtype)


def _glu(v, w_bf, b_glu, gate, tm):
    m, c = v.shape
    tn = _tile(c, 512, LANES)
    return pl.pallas_call(
        _glu_kernel,
        grid=(m // tm, c // tn),
        in_specs=[pl.BlockSpec((tm, c), lambda i, j: (i, 0)),
                  pl.BlockSpec((tm, tn), lambda i, j: (i, j)),
                  pl.BlockSpec((c, tn), lambda i, j: (0, j)),
                  pl.BlockSpec((1, tn), lambda i, j: (0, j)),
                  pl.BlockSpec((tm, tn), lambda i, j: (i, j))],
        out_specs=pl.BlockSpec((tm, tn), lambda i, j: (i, j)),
        out_shape=jax.ShapeDtypeStruct((m, c), BF16),
        scratch_shapes=[pltpu.VMEM((tm, c), BF16)],
        compiler_params=_params("parallel", "arbitrary"),
        name="ssm_glu",
    )(v, v, w_bf, b_glu.reshape(1, c), gate)


def _outproj_kernel(a_ref, s_ref, wa_ref, ws_ref, x_ref, g_ref, o_ref):
    acc = jnp.dot(a_ref[...], wa_ref[...], preferred_element_type=F32)
    acc += jnp.dot(s_ref[...], ws_ref[...], preferred_element_type=F32)
    o_ref[...] = x_ref[...] + g_ref[...] * acc


def _outproj(lhs_a, lhs_s, col_s, w_bf, x2, gate, rows_per_batch, tm, name):
    m = lhs_a.shape[0]
    kh = w_bf.shape[0] // 2
    d = w_bf.shape[1]
    tn = _tile(d, 512, LANES)
    tpb = rows_per_batch // tm
    assert rows_per_batch % tm == 0 and lhs_a.shape[1] % kh == 0 and lhs_s.shape[1] % kh == 0
    return pl.pallas_call(
        _outproj_kernel,
        grid=(m // tm, d // tn),
        in_specs=[pl.BlockSpec((tm, kh), lambda i, j: (i, 0)),
                  pl.BlockSpec((tm, kh), lambda i, j: (i, col_s)),
                  pl.BlockSpec((kh, tn), lambda i, j: (0, j)),
                  pl.BlockSpec((kh, tn), lambda i, j: (1, j)),
                  pl.BlockSpec((tm, tn), lambda i, j: (i, j)),
                  pl.BlockSpec((None, 1, tn), lambda i, j: (i // tpb, 0, j))],
        out_specs=pl.BlockSpec((tm, tn), lambda i, j: (i, j)),
        out_shape=jax.ShapeDtypeStruct((m, d), F32),
        compiler_params=_params("parallel", "parallel"),
        name=name,
    )(lhs_a, lhs_s, w_bf, w_bf, x2, gate)


def _inproj1_kernel(tpb, x_ref, xp_ref, xq_ref, nw_ref, sh_ref, sc_ref, wb_ref, wc_ref, wx_ref, wg_ref,
                    cw_ref, cb_ref, o_ref, xn_scr):
    i = pl.program_id(0)
    tm = x_ref.shape[0]

    @pl.when(pl.program_id(1) == 0)
    def _():
        nw, sh, sc = nw_ref[...], sh_ref[...], sc_ref[...]
        xn_scr[0:HALO, :] = _modulated(xp_ref[...], nw, sh, sc).astype(BF16)
        slab = min(tm, 256)
        for r in range(0, tm, slab):
            xn_scr[HALO + r:HALO + r + slab, :] = _modulated(x_ref[r:r + slab, :], nw, sh, sc).astype(BF16)
        xn_scr[HALO + tm:, :] = _modulated(xq_ref[...], nw, sh, sc).astype(BF16)

    xn_all = xn_scr[...]
    xn = xn_scr[HALO:HALO + tm, :]
    c_gate = jnp.dot(xn_all, wc_ref[...], preferred_element_type=F32)
    xin = jnp.dot(xn_all, wx_ref[...], preferred_element_type=F32)
    b_gate = jnp.dot(xn, wb_ref[...], preferred_element_type=F32)
    g = jnp.dot(xn, wg_ref[...], preferred_element_type=F32)

    y = c_gate * xin
    rows = y.shape[0]
    row = lax.broadcasted_iota(jnp.int32, y.shape, 0)
    lo = jnp.where((i % tpb) == 0, HALO, 0)
    hi = jnp.where((i % tpb) == tpb - 1, HALO + tm, rows)
    y = jnp.where((row >= lo) & (row < hi), y, 0.0)
    y_prev = pltpu.roll(y, 1, 0)[HALO:HALO + tm]
    y_next = pltpu.roll(y, rows - 1, 0)[HALO:HALO + tm]
    conv = cb_ref[...] + y_prev * cw_ref[0:1, :] + y[HALO:HALO + tm] * cw_ref[1:2, :] + y_next * cw_ref[2:3, :]
    o_ref[...] = (b_gate * conv * _silu(g)).astype(o_ref.dtype)


def _inproj1(x2, norm_w, shift, scale, w_bf, conv_w, conv_b, rows_per_batch, tm):
    m, d = x2.shape
    c = w_bf.shape[1] // 4
    tn = _tile(c, 256, LANES)
    nj = c // tn
    tpb = rows_per_batch // tm
    hb = tm // HALO
    nhb = m // HALO
    assert rows_per_batch % tm == 0 and tm % HALO == 0

    def wspec(p):
        return pl.BlockSpec((d, tn), lambda i, j: (0, p * nj + j))

    return pl.pallas_call(
        functools.partial(_inproj1_kernel, tpb),
        grid=(m // tm, nj),
        in_specs=[pl.BlockSpec((tm, d), lambda i, j: (i, 0), pipeline_mode=pl.Buffered(1)),
                  pl.BlockSpec((HALO, d), lambda i, j: (jnp.maximum(i * hb - 1, 0), 0)),
                  pl.BlockSpec((HALO, d), lambda i, j: (jnp.minimum((i + 1) * hb, nhb - 1), 0)),
                  pl.BlockSpec((1, d), lambda i, j: (0, 0)),
                  pl.BlockSpec((None, 1, d), lambda i, j: (i // tpb, 0, 0)),
                  pl.BlockSpec((None, 1, d), lambda i, j: (i // tpb, 0, 0)),
                  wspec(0), wspec(1), wspec(2), wspec(3),
                  pl.BlockSpec((CONV_K, tn), lambda i, j: (0, j)),
                  pl.BlockSpec((1, tn), lambda i, j: (0, j))],
        out_specs=pl.BlockSpec((tm, tn), lambda i, j: (i, j)),
        out_shape=jax.ShapeDtypeStruct((m, c), BF16),
        scratch_shapes=[pltpu.VMEM((tm + 2 * HALO, d), BF16)],
        compiler_params=_params("parallel", "arbitrary"),
        name="inproj1",
    )(x2, x2, x2, norm_w.reshape(1, d), shift, scale, w_bf, w_bf, w_bf, w_bf, conv_w, conv_b.reshape(1, c))


def _rope_tables(n_tok):
    rows = n_tok // GRID_W
    r, col = jnp.meshgrid(jnp.arange(rows, dtype=F32), jnp.arange(GRID_W, dtype=F32), indexing="ij")
    axis_dim = HEAD_DIM // 2
    inv_freq = ROPE_THETA ** (-jnp.arange(0, axis_dim, 2, dtype=F32) / axis_dim)
    ar = r.reshape(-1)[:, None] * inv_freq
    ac = col.reshape(-1)[:, None] * inv_freq
    cos_t = jnp.concatenate([jnp.cos(ar), jnp.cos(ar), jnp.cos(ac), jnp.cos(ac)], axis=1)
    sin_t = jnp.concatenate([-jnp.sin(ar), jnp.sin(ar), -jnp.sin(ac), jnp.sin(ac)], axis=1)
    return cos_t, sin_t


def kernel(x, c, ctx, c_ctx, l0_norm_w, l0_w_mod, l0_b_mod, l0_w_in, l0_q_norm_w, l0_k_norm_w, l0_fwd_lam_re, l0_fwd_lam_im, l0_fwd_log_dt, l0_fwd_b_re, l0_fwd_b_im, l0_fwd_c_re, l0_fwd_c_im, l0_bwd_lam_re, l0_bwd_lam_im, l0_bwd_log_dt, l0_bwd_b_re, l0_bwd_b_im, l0_bwd_c_re, l0_bwd_c_im, l0_ssm_d, l0_w_glu, l0_b_glu, l0_w_out, l1_norm_w, l1_w_mod, l1_b_mod, l1_w_in, l1_conv_w, l1_conv_b, l1_w_out):
    b, l, d = x.shape
    lc = ctx.shape[1]
    ssm_w = l0_ssm_d.shape[0]
    att_w = l0_w_out.shape[0] - ssm_w
    kv_w = att_w // Q_PER_KV
    widths = (att_w, kv_w, kv_w, att_w, ssm_w, ssm_w)
    assert sum(widths) == l0_w_in.shape[1]

    pad = (-(b + 1)) % 16
    cond = jnp.concatenate([c, c_ctx[None, :], jnp.zeros((pad, d), F32)], axis=0)
    mod0 = _ada_mod(cond, l0_w_mod, l0_b_mod)
    mod1 = _ada_mod(cond, l1_w_mod, l1_b_mod)
    shift0, scale0, gate0 = (mod0[:b, k * d:(k + 1) * d].reshape(b, 1, d) for k in range(3))
    shift0c, scale0c = (jnp.broadcast_to(mod0[b, k * d:(k + 1) * d].reshape(1, 1, d), (b, 1, d)) for k in range(2))
    shift1, scale1, gate1 = (mod1[:b, k * d:(k + 1) * d].reshape(b, 1, d) for k in range(3))

    x2 = x.reshape(b * l, d)
    ctx2 = ctx.reshape(b * lc, d)
    w_in0 = l0_w_in.astype(BF16)

    cos_t, sin_t = _rope_tables(l)
    tm = _tile(l, 512, SUBLANES)
    q, k, v, g_att, u, g_ssm, w_glu, w_out0, w_in1, w_out1 = _inproj0(
        x2, l0_norm_w, shift0, scale0, w_in0, cos_t, sin_t, l0_q_norm_w, l0_k_norm_w, widths, l, tm,
        side_casts=(l0_w_glu, l0_w_out, l1_w_in, l1_w_out))
    tmc = _tile(lc, 256, SUBLANES)
    ones_t, zeros_t = jnp.ones((lc, HEAD_DIM), F32), jnp.zeros((lc, HEAD_DIM), F32)
    _, kc, vc, _, uc, _ = _inproj0(ctx2, l0_norm_w, shift0c, scale0c, w_in0, ones_t, zeros_t,
                                   l0_q_norm_w, l0_k_norm_w, widths, lc, tmc)

    k_all = jnp.concatenate([kc.reshape(b, lc, kv_w), k.reshape(b, l, kv_w)], axis=1)
    v_all = jnp.concatenate([vc.reshape(b, lc, kv_w), v.reshape(b, l, kv_w)], axis=1)
    score_bound = HEAD_DIM ** 0.5 * jnp.max(jnp.abs(l0_q_norm_w)) * jnp.max(jnp.abs(l0_k_norm_w))
    att = lax.cond(score_bound < SCORE_BOUND_NO_SHIFT,
                   functools.partial(_attention_noshift, tq=_tile(l, 512, SUBLANES)),
                   functools.partial(_attention, tq=_tile(l, 512, SUBLANES)),
                   q.reshape(b, l, att_w), k_all, v_all, g_att.reshape(b, l, att_w))

    fwd = (l0_fwd_lam_re, l0_fwd_lam_im, l0_fwd_log_dt, l0_fwd_b_re, l0_fwd_b_im, l0_fwd_c_re, l0_fwd_c_im)
    bwd = (l0_bwd_lam_re, l0_bwd_lam_im, l0_bwd_log_dt, l0_bwd_b_re, l0_bwd_b_im, l0_bwd_c_re, l0_bwd_c_im)
    n_in = l0_fwd_b_re.shape[2]
    wf, vf, ktf, af = _s5_params(fwd, False, jnp.zeros((ssm_w // n_in, n_in, n_in), F32))
    wb, vb, ktb, ab = _s5_params(bwd, True, ktf[:, -n_in:, :])
    toep = _toeplitz(ktf, ktb).astype(BF16)
    tmo = _tile(l, 1024, SUBLANES)
    act = _s5(u.reshape(b, l, ssm_w), uc.reshape(b, lc, ssm_w), l0_ssm_d, toep, wf, wb, vf, vb, af, ab)
    ssm = _glu(act.reshape(b * l, ssm_w), w_glu, l0_b_glu, g_ssm, tmo)

    assert att_w == ssm_w
    h1 = _outproj(att.reshape(b * l, att_w), ssm, 0, w_out0, x2, gate0, l, tmo, "outproj0")

    mix1 = _inproj1(h1, l1_norm_w, shift1, scale1, w_in1, l1_conv_w, l1_conv_b, l, tmo)
    h2 = _outproj(mix1, mix1, 1, w_out1, h1, gate1, l, tmo, "outproj1")
    return h2.reshape(b, l, d)
```

```python
import functools
import math

import jax
import jax.numpy as jnp
import numpy as np
from jax import lax
from jax.experimental import pallas as pl
from jax.experimental.pallas import tpu as pltpu

F32 = jnp.float32
BF16 = jnp.bfloat16

EPS = 1e-6
HEAD_DIM = 128
Q_PER_KV = 4
GRID_W = 64
ROPE_THETA = 10000.0
CONV_K = 3
Q_SCALE = HEAD_DIM ** -0.5 * math.log2(math.e)
SCORE_BOUND_NO_SHIFT = 40.0

LANES = 128
SUBLANES = 8
HALO = 16
CAST_ROWS = 16
S5_CHUNK = 16
S5_POW_ROWS = 32
S5_TAB_ROWS = 10 * SUBLANES
VMEM_LIMIT_BYTES = 56 * 1024 * 1024


def _params(*semantics):
    return pltpu.CompilerParams(dimension_semantics=semantics, vmem_limit_bytes=VMEM_LIMIT_BYTES)


def _tile(dim, pref, mult):
    if dim <= pref:
        return dim
    t = (pref // mult) * mult
    while t > mult and dim % t:
        t -= mult
    assert dim % t == 0, (dim, pref, mult)
    return t


def _side_cast_specs(mats, steps, step_of):
    specs = []
    for a in mats:
        slab = CAST_ROWS * -(-a.shape[0] // (CAST_ROWS * steps))
        nblk = a.shape[0] // slab
        assert a.shape[0] % slab == 0 and nblk <= steps, (a.shape, steps)
        specs.append(pl.BlockSpec((slab, a.shape[1]),
                                  lambda *idx, nblk=nblk: (jnp.minimum(step_of(*idx), nblk - 1), 0)))
    return specs


def _side_cast(cast_in, cast_out):
    for src, dst in zip(cast_in, cast_out):
        dst[...] = src[...].astype(BF16)


def _sigmoid(x):
    return 1.0 / (1.0 + jnp.exp(-x))


def _silu(x):
    return x * _sigmoid(x)


def _ada_kernel(c_ref, w_ref, b_ref, o_ref):
    s = _silu(c_ref[...]).astype(BF16)
    o_ref[...] = jnp.dot(s, w_ref[...].astype(BF16), preferred_element_type=F32) + b_ref[...]


def _ada_mod(cond, w_mod, b_mod):
    rows, d = cond.shape
    n = w_mod.shape[1]
    tn = _tile(n, 512, LANES)
    return pl.pallas_call(
        _ada_kernel,
        grid=(n // tn,),
        in_specs=[pl.BlockSpec((rows, d), lambda j: (0, 0)),
                  pl.BlockSpec((d, tn), lambda j: (0, j)),
                  pl.BlockSpec((1, tn), lambda j: (0, j))],
        out_specs=pl.BlockSpec((rows, tn), lambda j: (0, j)),
        out_shape=jax.ShapeDtypeStruct((rows, n), F32),
        compiler_params=_params("parallel"),
        name="ada_mod",
    )(cond, w_mod, b_mod.reshape(1, n))


def _modulated(x, nw, shift, scale):
    ms = jnp.mean(x * x, axis=-1, keepdims=True)
    return (x * lax.rsqrt(ms + EPS) * nw) * (1.0 + scale) + shift


def _norm_rope_heads(acc, nw, cos, sin, out_scale, o_ref):
    lane = lax.broadcasted_iota(jnp.int32, (acc.shape[0], HEAD_DIM), 1)
    first_half = (lane % 64) < 32
    for h in range(acc.shape[1] // HEAD_DIM):
        a = acc[:, h * HEAD_DIM:(h + 1) * HEAD_DIM]
        ms = jnp.mean(a * a, axis=-1, keepdims=True)
        a = a * lax.rsqrt(ms + EPS) * nw
        swapped = jnp.where(first_half, pltpu.roll(a, HEAD_DIM - 32, 1), pltpu.roll(a, 32, 1))
        o_ref[:, h * HEAD_DIM:(h + 1) * HEAD_DIM] = ((a * cos + swapped * sin) * out_scale).astype(o_ref.dtype)


def _inproj0_kernel(bounds, x_ref, nw_ref, sh_ref, sc_ref, w_ref, cos_ref, sin_ref, qn_ref, kn_ref,
                    q_ref, k_ref, v_ref, ga_ref, u_ref, gs_ref, xn_scr, acc_scr):
    j = pl.program_id(1)
    e_q, e_k, e_v, e_ga, e_u = bounds

    def dot():
        return jnp.dot(xn_scr[...], w_ref[...], preferred_element_type=F32)

    @pl.when(j == 0)
    def _():
        xn_scr[...] = _modulated(x_ref[...], nw_ref[...], sh_ref[...], sc_ref[...]).astype(BF16)
        acc_scr[0] = dot()

    @pl.when((j >= 1) & (j <= e_q))
    def _():
        _norm_rope_heads(acc_scr[(j - 1) % 2], qn_ref[...], cos_ref[...], sin_ref[...], Q_SCALE, q_ref)
        acc_scr[j % 2] = dot()

    @pl.when(j == e_k)
    def _():
        _norm_rope_heads(acc_scr[(j - 1) % 2], kn_ref[...], cos_ref[...], sin_ref[...], 1.0, k_ref)
        v_ref[...] = dot().astype(BF16)

    @pl.when((j >= e_v) & (j < e_ga))
    def _():
        ga_ref[...] = _silu(dot()).astype(BF16)

    @pl.when((j >= e_ga) & (j < e_u))
    def _():
        u_ref[...] = dot()

    @pl.when(j >= e_u)
    def _():
        gs_ref[...] = _silu(dot()).astype(BF16)


def _inproj0(x2, norm_w, shift, scale, w_bf, cos_t, sin_t, qn, kn, widths, rows_per_batch, tm):
    m, d = x2.shape
    n = w_bf.shape[1]
    tn = _tile(min(widths), 512, LANES)
    assert all(w % tn == 0 for w in widths) and sum(widths) == n
    ends, off = [], 0
    for w in widths:
        off += w
        ends.append(off // tn)
    starts = [0] + ends[:-1]
    assert ends[1] - starts[1] == 1 and ends[2] - starts[2] == 1, "k and v must be one column tile each"
    tpb = rows_per_batch // tm
    assert rows_per_batch % tm == 0 and m % tm == 0

    def out_spec(p):
        lo, hi = starts[p], ends[p]
        late = 1 if p == 0 else 0
        return pl.BlockSpec((tm, tn), lambda i, j: (i, jnp.clip(j - late - lo, 0, hi - lo - 1)))

    dts = (BF16, BF16, BF16, BF16, F32, BF16)
    return pl.pallas_call(
        functools.partial(_inproj0_kernel, tuple(ends[:5])),
        grid=(m // tm, n // tn),
        in_specs=[pl.BlockSpec((tm, d), lambda i, j: (i, 0)),
                  pl.BlockSpec((1, d), lambda i, j: (0, 0)),
                  pl.BlockSpec((None, 1, d), lambda i, j: (i // tpb, 0, 0)),
                  pl.BlockSpec((None, 1, d), lambda i, j: (i // tpb, 0, 0)),
                  pl.BlockSpec((d, tn), lambda i, j: (0, j)),
                  pl.BlockSpec((tm, HEAD_DIM), lambda i, j: (i % tpb, 0)),
                  pl.BlockSpec((tm, HEAD_DIM), lambda i, j: (i % tpb, 0)),
                  pl.BlockSpec((1, HEAD_DIM), lambda i, j: (0, 0)),
                  pl.BlockSpec((1, HEAD_DIM), lambda i, j: (0, 0))],
        out_specs=[out_spec(p) for p in range(6)],
        out_shape=[jax.ShapeDtypeStruct((m, w), dt) for w, dt in zip(widths, dts)],
        scratch_shapes=[pltpu.VMEM((tm, d), BF16), pltpu.VMEM((2, tm, tn), F32)],
        compiler_params=_params("parallel", "arbitrary"),
        name="inproj0",
    )(x2, norm_w.reshape(1, d), shift, scale, w_bf, cos_t, sin_t, qn.reshape(1, HEAD_DIM), kn.reshape(1, HEAD_DIM))


def _ctxproj_kernel(x_ref, nw_ref, sh_ref, sc_ref, w_ref, kn_ref, k_ref, v_ref, u_ref, xn_scr):
    j = pl.program_id(0)

    @pl.when(j == 0)
    def _():
        xn_scr[...] = _modulated(x_ref[...], nw_ref[...], sh_ref[...], sc_ref[...]).astype(BF16)

    acc = jnp.dot(xn_scr[...], w_ref[...], preferred_element_type=F32)

    @pl.when(j == 0)
    def _():
        for h in range(acc.shape[1] // HEAD_DIM):
            a = acc[:, h * HEAD_DIM:(h + 1) * HEAD_DIM]
            ms = jnp.mean(a * a, axis=-1, keepdims=True)
            k_ref[:, h * HEAD_DIM:(h + 1) * HEAD_DIM] = (a * lax.rsqrt(ms + EPS) * kn_ref[...]).astype(BF16)

    @pl.when(j == 1)
    def _():
        v_ref[...] = acc.astype(BF16)

    @pl.when(j >= 2)
    def _():
        u_ref[...] = acc


def _ctxproj(ctx2, norm_w, shift_c, scale_c, w_bf, kn, widths):
    mc, d = ctx2.shape
    tn = widths[1]
    assert widths[2] == tn and widths[0] % tn == 0 and widths[3] % tn == 0 and widths[4] % tn == 0
    k_tile = widths[0] // tn
    u_tile = (widths[0] + 2 * tn + widths[3]) // tn
    nu = widths[4] // tn
    one = lambda j: (0, 0)
    return pl.pallas_call(
        _ctxproj_kernel,
        grid=(2 + nu,),
        in_specs=[pl.BlockSpec((mc, d), one), pl.BlockSpec((1, d), one), pl.BlockSpec((1, d), one),
                  pl.BlockSpec((1, d), one),
                  pl.BlockSpec((d, tn), lambda j: (0, jnp.where(j < 2, k_tile + j, u_tile + j - 2))),
                  pl.BlockSpec((1, HEAD_DIM), one)],
        out_specs=[pl.BlockSpec((mc, tn), one), pl.BlockSpec((mc, tn), one),
                   pl.BlockSpec((mc, tn), lambda j: (0, jnp.maximum(j - 2, 0)))],
        out_shape=[jax.ShapeDtypeStruct((mc, tn), BF16), jax.ShapeDtypeStruct((mc, tn), BF16),
                   jax.ShapeDtypeStruct((mc, widths[4]), F32)],
        scratch_shapes=[pltpu.VMEM((mc, d), BF16)],
        compiler_params=_params("arbitrary"),
        name="ctxproj",
    )(ctx2, norm_w.reshape(1, d), shift_c, scale_c, w_bf, kn.reshape(1, HEAD_DIM))


def _attn_kernel(q_ref, k_ref, v_ref, g_ref, o_ref, m_scr, l_scr, acc_scr):
    kv = pl.program_id(3)

    @pl.when(kv == 0)
    def _():
        m_scr[...] = jnp.full(m_scr.shape, -jnp.inf, F32)
        l_scr[...] = jnp.zeros(l_scr.shape, F32)
        acc_scr[...] = jnp.zeros(acc_scr.shape, F32)

    k = k_ref[...]
    v = v_ref[...]
    for h in range(Q_PER_KV):
        qh = q_ref[:, h * HEAD_DIM:(h + 1) * HEAD_DIM]
        s = lax.dot_general(qh, k, (((1,), (1,)), ((), ())), preferred_element_type=F32)
        m_prev = m_scr[h]
        m_new = jnp.maximum(m_prev, jnp.max(s, axis=1, keepdims=True))
        alpha = jnp.exp2(m_prev - m_new)
        p = jnp.exp2(s - m_new[:, :1])
        l_scr[h] = alpha * l_scr[h] + jnp.sum(p, axis=1, keepdims=True)
        acc_scr[h] = alpha * acc_scr[h] + jnp.dot(p.astype(BF16), v, preferred_element_type=F32)
        m_scr[h] = m_new

    @pl.when(kv == pl.num_programs(3) - 1)
    def _():
        for h in range(Q_PER_KV):
            sl = slice(h * HEAD_DIM, (h + 1) * HEAD_DIM)
            o = acc_scr[h] / l_scr[h]
            o_ref[:, sl] = (o * g_ref[:, sl].astype(F32)).astype(o_ref.dtype)


def _attn_noshift_kernel(q_ref, k_ref, v_ref, g_ref, o_ref):
    tq = q_ref.shape[0]
    q = jnp.concatenate([q_ref[:, h * HEAD_DIM:(h + 1) * HEAD_DIM] for h in range(Q_PER_KV)], axis=0)
    s = lax.dot_general(q, k_ref[...], (((1,), (1,)), ((), ())), preferred_element_type=F32)
    p = jnp.exp2(s).astype(BF16)
    v = v_ref[...]
    v_ext = jnp.concatenate([v, jnp.ones_like(v)], axis=1)
    a = jnp.dot(p, v_ext, preferred_element_type=F32)
    for h in range(Q_PER_KV):
        sl = slice(h * HEAD_DIM, (h + 1) * HEAD_DIM)
        ah = a[h * tq:(h + 1) * tq]
        o = ah[:, :HEAD_DIM] / ah[:, HEAD_DIM:]
        o_ref[:, sl] = (o * g_ref[:, sl].astype(F32)).astype(o_ref.dtype)


def _attention_noshift(q, k_all, v_all, gate, tq):
    b, l, aw = q.shape
    s_len, kvw = k_all.shape[1], k_all.shape[2]
    kvh = kvw // HEAD_DIM
    gw = Q_PER_KV * HEAD_DIM
    assert aw == kvh * gw
    return pl.pallas_call(
        _attn_noshift_kernel,
        grid=(b, kvh, l // tq),
        in_specs=[pl.BlockSpec((None, tq, gw), lambda bi, h, qi: (bi, qi, h)),
                  pl.BlockSpec((None, s_len, HEAD_DIM), lambda bi, h, qi: (bi, 0, h)),
                  pl.BlockSpec((None, s_len, HEAD_DIM), lambda bi, h, qi: (bi, 0, h)),
                  pl.BlockSpec((None, tq, gw), lambda bi, h, qi: (bi, qi, h))],
        out_specs=pl.BlockSpec((None, tq, gw), lambda bi, h, qi: (bi, qi, h)),
        out_shape=jax.ShapeDtypeStruct((b, l, aw), BF16),
        compiler_params=_params("parallel", "parallel", "parallel"),
        name="attention_noshift",
    )(q, k_all, v_all, gate)


def _attention(q, k_all, v_all, gate, tq):
    tk = _tile(k_all.shape[1], 1408, LANES)
    b, l, aw = q.shape
    s_len, kvw = k_all.shape[1], k_all.shape[2]
    kvh = kvw // HEAD_DIM
    gw = Q_PER_KV * HEAD_DIM
    assert aw == kvh * gw
    return pl.pallas_call(
        _attn_kernel,
        grid=(b, kvh, l // tq, s_len // tk),
        in_specs=[pl.BlockSpec((None, tq, gw), lambda bi, h, qi, ki: (bi, qi, h)),
                  pl.BlockSpec((None, tk, HEAD_DIM), lambda bi, h, qi, ki: (bi, ki, h)),
                  pl.BlockSpec((None, tk, HEAD_DIM), lambda bi, h, qi, ki: (bi, ki, h)),
                  pl.BlockSpec((None, tq, gw), lambda bi, h, qi, ki: (bi, qi, h))],
        out_specs=pl.BlockSpec((None, tq, gw), lambda bi, h, qi, ki: (bi, qi, h)),
        out_shape=jax.ShapeDtypeStruct((b, l, aw), BF16),
        scratch_shapes=[pltpu.VMEM((Q_PER_KV, tq, HEAD_DIM), F32)] * 3,
        compiler_params=_params("parallel", "parallel", "parallel", "arbitrary"),
        name="attention",
    )(q, k_all, v_all, gate)


def _s5_param_kernel(reverse, lr_ref, li_ref, ldt_ref, b1_ref, bsw_ref, c1_ref, csw_ref, k0_ref,
                     w_ref, vt_ref, kt_ref, a_ref):
    for gi in range(lr_ref.shape[0]):
        _s5_group_params(reverse, lr_ref.at[gi], li_ref.at[gi], ldt_ref.at[gi], b1_ref.at[gi], bsw_ref.at[gi],
                         c1_ref.at[gi], csw_ref.at[gi], k0_ref.at[gi],
                         w_ref.at[gi], vt_ref.at[gi], kt_ref.at[gi], a_ref.at[gi])


def _s5_group_params(reverse, lr_ref, li_ref, ldt_ref, b1_ref, bsw_ref, c1_ref, csw_ref, k0_ref,
                     w_ref, vt_ref, kt_ref, a_ref):
    t = S5_CHUNK
    lr, li = lr_ref[...], li_ref[...]
    dt = jnp.exp(ldt_ref[...])
    lane = lax.broadcasted_iota(jnp.int32, (1, LANES), 1)
    neg_first = jnp.where(lane < 64, -1.0, 1.0).astype(F32)
    jj = lax.broadcasted_iota(jnp.int32, (S5_POW_ROWS, LANES), 0).astype(F32)
    mag = jnp.exp(lr * dt * jj)
    ang = li * dt * jj
    p_re = mag * jnp.cos(ang)
    p_im = mag * jnp.sin(ang)

    a_re, a_im = p_re[1:2], p_im[1:2]
    den = lr * lr + li * li
    n_re = a_re - 1.0
    coef_re = (n_re * lr + a_im * li) / den
    coef_im = (a_im * lr - n_re * li) / den
    b1 = b1_ref[...]
    b2 = bsw_ref[...] * neg_first
    bb1 = coef_re * b1 + coef_im * b2
    bb2 = coef_re * b2 - coef_im * b1

    c1 = c1_ref[...] * (-neg_first)
    csw = csw_ref[...]

    xs = [None] * t
    for j in range(t):
        x_j = p_re[j:j + 1] * bb1 + p_im[j:j + 1] * bb2
        blk = (t - 1 - j) if not reverse else j
        xs[blk] = x_j
        w_ref[blk * 16:(blk + 1) * 16, 0:LANES] = x_j.astype(w_ref.dtype)
        w_ref[blk * 16:(blk + 1) * 16, LANES:2 * LANES] = pltpu.roll(x_j, LANES // 2, 1).astype(w_ref.dtype)
        m = (j + 1) if not reverse else (t - j)
        vt_ref[j * 16:(j + 1) * 16, :] = (p_re[m:m + 1] * c1 - p_im[m:m + 1] * csw).astype(vt_ref.dtype)
    x_all = jnp.concatenate(xs, axis=0)
    kt = lax.dot_general(x_all, c1, (((1,), (1,)), ((), ())),
                         precision=lax.Precision.HIGHEST, preferred_element_type=F32)
    kt_ref[...] = kt
    lag0 = 0 if reverse else (t - 1) * 16
    kt_ref[lag0:lag0 + 16, :] = kt[lag0:lag0 + 16] + k0_ref[...]

    r = lax.broadcasted_iota(jnp.int32, (SUBLANES, LANES), 0)
    last = SUBLANES - 1

    def table(k, expo, keep):
        e = expo.astype(F32) * float(t)
        mg = jnp.exp(lr * dt * e)
        an = li * dt * e
        a_ref[2 * k * SUBLANES:(2 * k + 1) * SUBLANES, :] = jnp.where(keep, mg * jnp.cos(an), 0.0)
        a_ref[(2 * k + 1) * SUBLANES:(2 * k + 2) * SUBLANES, :] = jnp.where(keep, mg * jnp.sin(an) * neg_first, 0.0)

    for k, d in enumerate((1, 2, 4)):
        table(k, jnp.full_like(r, d), (r <= last - d) if reverse else (r >= d))
    table(3, (last - r) if reverse else r, r >= 0)
    table(4, jnp.full_like(r, SUBLANES), r >= 0)


def _s5_params(prm, reverse, k0):
    lam_re, lam_im, log_dt, b_re, b_im, c_re, c_im = prm
    g, p = lam_re.shape
    i = b_re.shape[2]
    assert 2 * p == LANES and i * S5_CHUNK == 256

    def dup(a):
        return jnp.concatenate([a, a], axis=-1).reshape(g, 1, LANES).astype(F32)

    bt_re, bt_im = jnp.swapaxes(b_re, 1, 2).astype(F32), jnp.swapaxes(b_im, 1, 2).astype(F32)
    b1 = jnp.concatenate([bt_re, bt_im], axis=-1)
    bsw = jnp.concatenate([bt_im, bt_re], axis=-1)
    c1 = jnp.concatenate([c_re, c_im], axis=-1).astype(F32)
    csw = jnp.concatenate([c_im, c_re], axis=-1).astype(F32)
    ldt = jnp.broadcast_to(log_dt.astype(F32)[:, None, None], (g, 1, LANES))
    gb = 8 if g % 8 == 0 else 1

    def spec(rows, cols):
        return pl.BlockSpec((gb, rows, cols), lambda gi: (gi, 0, 0))

    return pl.pallas_call(
        functools.partial(_s5_param_kernel, reverse),
        grid=(g // gb,),
        in_specs=[spec(1, LANES)] * 3 + [spec(i, LANES)] * 4 + [spec(i, i)],
        out_specs=[spec(256, 2 * LANES), spec(256, LANES), spec(256, i), spec(S5_TAB_ROWS, LANES)],
        out_shape=[jax.ShapeDtypeStruct((g, 256, 2 * LANES), BF16),
                   jax.ShapeDtypeStruct((g, 256, LANES), BF16),
                   jax.ShapeDtypeStruct((g, 256, i), F32),
                   jax.ShapeDtypeStruct((g, S5_TAB_ROWS, LANES), F32)],
        compiler_params=_params("parallel"),
        name="s5_params",
    )(dup(lam_re), dup(lam_im), ldt, b1, bsw, c1, csw, k0)


def _toeplitz(kt_f, kt_b):
    t = S5_CHUNK
    g, rows, i = kt_f.shape
    z = jnp.concatenate([kt_f[:, :rows - i], kt_b], axis=1)
    cols = [z[:, (t - 1 - tt) * i:(t - 1 - tt) * i + rows] for tt in range(t)]
    return jnp.stack(cols, axis=2).reshape(g, rows, t * i)


def _slot_transpose8(vs, slot):
    vs = list(vs)
    for d in (4, 2, 1):
        keep = (slot & d) == 0
        new = list(vs)
        for r in range(8):
            if r & d == 0:
                a, b = vs[r], vs[r + d]
                new[r] = jnp.where(keep, a, pltpu.roll(b, 16 * d, 1))
                new[r + d] = jnp.where(keep, pltpu.roll(a, LANES - 16 * d, 1), b)
        vs = new
    return vs


def _gather_chunks(src_ref, dst_scr, n, rb):
    t = S5_CHUNK
    for r0 in range(0, n, rb):
        slot = lax.broadcasted_iota(jnp.int32, (rb, LANES), 1) // 16
        for hi in range(2):
            pieces = [src_ref[pl.ds(r0 * t + hi * 8 + r, rb, stride=t), :] for r in range(8)]
            outs = _slot_transpose8(pieces, slot)
            for gi in range(8):
                dst_scr[gi, r0:r0 + rb, hi * LANES:(hi + 1) * LANES] = outs[gi].astype(BF16)


def _s5_kernel(n, nc, rb, n_cast, ux_ref, uc_ref, d_ref, toep_ref, wf_ref, wb_ref, vf_ref, vb_ref, af_ref, ab_ref,
               *rest):
    cast_in, y_ref, cast_out = rest[:n_cast], rest[n_cast], rest[n_cast + 1:2 * n_cast + 1]
    u_scr, ucx_scr, wf_scr, wfs_scr, wb_scr, wbs_scr, hf_scr, gb_scr = rest[2 * n_cast + 1:]
    _side_cast(cast_in, cast_out)
    t = S5_CHUNK
    ntot = n + nc
    nblk = ntot // SUBLANES
    width = 8 * LANES
    _gather_chunks(ux_ref, u_scr, n, rb)
    _gather_chunks(uc_ref, ucx_scr, nc, min(rb, nc))

    for gi in range(8):
        ls = slice(gi * LANES, (gi + 1) * LANES)
        ug, ucg = u_scr[gi], ucx_scr[gi]
        for w_ref, x_rows, c_rows, dst, dst_s in ((wf_ref, slice(nc, ntot), slice(0, nc), wf_scr, wfs_scr),
                                                  (wb_ref, slice(0, n), slice(n, ntot), wb_scr, wbs_scr)):
            w_x = jnp.dot(ug, w_ref[gi], preferred_element_type=F32)
            w_c = jnp.dot(ucg, w_ref[gi], preferred_element_type=F32)
            dst[x_rows, ls], dst_s[x_rows, ls] = w_x[:, :LANES], w_x[:, LANES:]
            dst[c_rows, ls], dst_s[c_rows, ls] = w_c[:, :LANES], w_c[:, LANES:]

    def tables(a_ref):
        return [jnp.concatenate([a_ref[gi, k * SUBLANES:(k + 1) * SUBLANES, :] for gi in range(8)], axis=1)
                for k in range(S5_TAB_ROWS // SUBLANES)]

    tab_f, tab_b = tables(af_ref), tables(ab_ref)
    row = lax.broadcasted_iota(jnp.int32, (SUBLANES, width), 0)

    def block(src, src_s, dst, r0, h, hs, tab, reverse):
        x, xs = src[pl.ds(r0, SUBLANES), :], src_s[pl.ds(r0, SUBLANES), :]
        for k, d in enumerate((1, 2, 4)):
            sh = SUBLANES - d if reverse else d
            sx, sxs = pltpu.roll(x, sh, 0), pltpu.roll(xs, sh, 0)
            x, xs = x + tab[2 * k] * sx + tab[2 * k + 1] * sxs, xs + tab[2 * k] * sxs - tab[2 * k + 1] * sx
        before = pltpu.roll(x, SUBLANES - 1 if reverse else 1, 0)
        before = jnp.where(row == (SUBLANES - 1 if reverse else 0), 0.0, before)
        hb, hsb = jnp.broadcast_to(h, x.shape), jnp.broadcast_to(hs, x.shape)
        dst[pl.ds(r0, SUBLANES), :] = before + tab[6] * hb + tab[7] * hsb
        e = 0 if reverse else SUBLANES - 1
        q1, q2 = tab[8][0:1], tab[9][0:1]
        return x[e:e + 1] + q1 * h + q2 * hs, xs[e:e + 1] + q1 * hs - q2 * h

    def step(i, carry):
        h, hs, g, gs = carry
        h, hs = block(wf_scr, wfs_scr, hf_scr, pl.multiple_of(i * SUBLANES, SUBLANES), h, hs, tab_f, False)
        g, gs = block(wb_scr, wbs_scr, gb_scr, pl.multiple_of((nblk - 1 - i) * SUBLANES, SUBLANES), g, gs,
                      tab_b, True)
        return h, hs, g, gs

    zero = jnp.zeros((1, width), F32)
    lax.fori_loop(0, nblk, step, (zero, zero, zero, zero))

    nt = (((1,), (1,)), ((), ()))
    for r0 in range(0, n, rb):
        slot = lax.broadcasted_iota(jnp.int32, (rb, LANES), 1) // 16
        ys = []
        for gi in range(8):
            ls = slice(gi * LANES, (gi + 1) * LANES)
            y = jnp.dot(u_scr[gi, r0:r0 + rb, :], toep_ref[gi], preferred_element_type=F32)
            y += lax.dot_general(hf_scr[nc + r0:nc + r0 + rb, ls].astype(BF16), vf_ref[gi], nt,
                                 preferred_element_type=F32)
            y += lax.dot_general(gb_scr[r0:r0 + rb, ls].astype(BF16), vb_ref[gi], nt,
                                 preferred_element_type=F32)
            ys.append(y)
        for hi in range(2):
            outs = _slot_transpose8([ys[gi][:, hi * LANES:(hi + 1) * LANES] for gi in range(8)], slot)
            for r in range(8):
                rows = pl.ds(r0 * t + hi * 8 + r, rb, stride=t)
                y_ref[rows, :] = jax.nn.gelu(outs[r] + d_ref[...] * ux_ref[rows, :])


def _s5(u, uc, d_skip, toep, wf2, wb2, vf, vb, af, ab, side_casts=()):
    b, l, c = u.shape
    lc = uc.shape[1]
    t = S5_CHUNK
    n, nc = l // t, lc // t
    assert l % t == 0 and lc % t == 0 and c % LANES == 0 and n % SUBLANES == 0 and nc % SUBLANES == 0
    rb = _tile(n, 128, SUBLANES)
    ntot = n + nc
    gmat = lambda rows, last: pl.BlockSpec((8, rows, last), lambda bi, ci: (ci, 0, 0))
    nci = c // LANES
    cast_specs = _side_cast_specs(side_casts, b * nci, lambda bi, ci: bi * nci + ci)
    out = pl.pallas_call(
        functools.partial(_s5_kernel, n, nc, rb, len(side_casts)),
        grid=(b, nci),
        in_specs=[pl.BlockSpec((None, l, LANES), lambda bi, ci: (bi, 0, ci)),
                  pl.BlockSpec((None, lc, LANES), lambda bi, ci: (bi, 0, ci)),
                  pl.BlockSpec((1, LANES), lambda bi, ci: (0, ci)),
                  gmat(256, 256), gmat(256, 256), gmat(256, 256), gmat(256, LANES), gmat(256, LANES),
                  gmat(S5_TAB_ROWS, LANES), gmat(S5_TAB_ROWS, LANES)] + cast_specs,
        out_specs=[pl.BlockSpec((None, l, LANES), lambda bi, ci: (bi, 0, ci))] + cast_specs,
        out_shape=[jax.ShapeDtypeStruct((b, l, c), F32)] + [jax.ShapeDtypeStruct(a.shape, BF16) for a in side_casts],
        scratch_shapes=[pltpu.VMEM((8, n, 256), BF16),
                        pltpu.VMEM((8, nc, 256), BF16)] + [pltpu.VMEM((ntot, 8 * LANES), F32)] * 6,
        compiler_params=_params("arbitrary", "arbitrary"),
        name="s5_scan",
    )(u, uc, d_skip.reshape(1, c), toep, wf2, wb2, vf, vb, af, ab, *side_casts)
    return out if side_casts else out[0]


def _glu_kernel(v_ref, vj_ref, w_ref, b_ref, g_ref, o_ref, vb_scr):
    @pl.when(pl.program_id(1) == 0)
    def _():
        vb_scr[...] = v_ref[...].astype(BF16)

    acc = jnp.dot(vb_scr[...], w_ref[...], preferred_element_type=F32) + b_ref[...]
    o_ref[...] = (vj_ref[...] * _sigmoid(acc) * g_ref[...].astype(F32)).astype(o_ref.dtype)


def _glu(v, w_bf, b_glu, gate, tm):
    m, c = v.shape
    tn = _tile(c, 512, LANES)
    return pl.pallas_call(
        _glu_kernel,
        grid=(m // tm, c // tn),
        in_specs=[pl.BlockSpec((tm, c), lambda i, j: (i, 0)),
                  pl.BlockSpec((tm, tn), lambda i, j: (i, j)),
                  pl.BlockSpec((c, tn), lambda i, j: (0, j)),
                  pl.BlockSpec((1, tn), lambda i, j: (0, j)),
                  pl.BlockSpec((tm, tn), lambda i, j: (i, j))],
        out_specs=pl.BlockSpec((tm, tn), lambda i, j: (i, j)),
        out_shape=jax.ShapeDtypeStruct((m, c), BF16),
        scratch_shapes=[pltpu.VMEM((tm, c), BF16)],
        compiler_params=_params("parallel", "arbitrary"),
        name="ssm_glu",
    )(v, v, w_bf, b_glu.reshape(1, c), gate)


def _outproj_kernel(n_cast, a_ref, s_ref, wa_ref, ws_ref, x_ref, g_ref, *rest):
    cast_in, o_ref, cast_out = rest[:n_cast], rest[n_cast], rest[n_cast + 1:]
    _side_cast(cast_in, cast_out)
    acc = jnp.dot(a_ref[...], wa_ref[...], preferred_element_type=F32)
    acc += jnp.dot(s_ref[...], ws_ref[...], preferred_element_type=F32)
    o_ref[...] = x_ref[...] + g_ref[...] * acc


def _outproj(lhs_a, lhs_s, col_s, w_bf, x2, gate, rows_per_batch, tm, name, side_casts=()):
    m = lhs_a.shape[0]
    kh = w_bf.shape[0] // 2
    d = w_bf.shape[1]
    tn = _tile(d, 512 if side_casts else 1024, LANES)
    nj = d // tn
    tpb = rows_per_batch // tm
    assert rows_per_batch % tm == 0 and lhs_a.shape[1] % kh == 0 and lhs_s.shape[1] % kh == 0
    cast_specs = _side_cast_specs(side_casts, (m // tm) * nj, lambda i, j: i * nj + j)
    out = pl.pallas_call(
        functools.partial(_outproj_kernel, len(side_casts)),
        grid=(m // tm, nj),
        in_specs=[pl.BlockSpec((tm, kh), lambda i, j: (i, 0)),
                  pl.BlockSpec((tm, kh), lambda i, j: (i, col_s)),
                  pl.BlockSpec((kh, tn), lambda i, j: (0, j)),
                  pl.BlockSpec((kh, tn), lambda i, j: (1, j)),
                  pl.BlockSpec((tm, tn), lambda i, j: (i, j)),
                  pl.BlockSpec((None, 1, tn), lambda i, j: (i // tpb, 0, j))] + cast_specs,
        out_specs=[pl.BlockSpec((tm, tn), lambda i, j: (i, j))] + cast_specs,
        out_shape=[jax.ShapeDtypeStruct((m, d), F32)] + [jax.ShapeDtypeStruct(a.shape, BF16) for a in side_casts],
        compiler_params=_params("arbitrary", "arbitrary"),
        name=name,
    )(lhs_a, lhs_s, w_bf, w_bf, x2, gate, *side_casts)
    return out if side_casts else out[0]


def _inproj1_kernel(tpb, x_ref, xp_ref, xq_ref, nw_ref, sh_ref, sc_ref, wb_ref, wc_ref, wx_ref, wg_ref,
                    cw_ref, cb_ref, o_ref, xn_scr):
    i = pl.program_id(0)
    tm = x_ref.shape[0]

    @pl.when(pl.program_id(1) == 0)
    def _():
        nw, sh, sc = nw_ref[...], sh_ref[...], sc_ref[...]
        xn_scr[0:HALO, :] = _modulated(xp_ref[...], nw, sh, sc).astype(BF16)
        slab = min(tm, 256)
        for r in range(0, tm, slab):
            xn_scr[HALO + r:HALO + r + slab, :] = _modulated(x_ref[r:r + slab, :], nw, sh, sc).astype(BF16)
        xn_scr[HALO + tm:, :] = _modulated(xq_ref[...], nw, sh, sc).astype(BF16)

    xn_all = xn_scr[...]
    xn = xn_scr[HALO:HALO + tm, :]
    c_gate = jnp.dot(xn_all, wc_ref[...], preferred_element_type=F32)
    xin = jnp.dot(xn_all, wx_ref[...], preferred_element_type=F32)
    b_gate = jnp.dot(xn, wb_ref[...], preferred_element_type=F32)
    g = jnp.dot(xn, wg_ref[...], preferred_element_type=F32)

    y = c_gate * xin
    rows = y.shape[0]
    row = lax.broadcasted_iota(jnp.int32, y.shape, 0)
    lo = jnp.where((i % tpb) == 0, HALO, 0)
    hi = jnp.where((i % tpb) == tpb - 1, HALO + tm, rows)
    y = jnp.where((row >= lo) & (row < hi), y, 0.0)
    y_prev = pltpu.roll(y, 1, 0)[HALO:HALO + tm]
    y_next = pltpu.roll(y, rows - 1, 0)[HALO:HALO + tm]
    conv = cb_ref[...] + y_prev * cw_ref[0:1, :] + y[HALO:HALO + tm] * cw_ref[1:2, :] + y_next * cw_ref[2:3, :]
    o_ref[...] = (b_gate * conv * _silu(g)).astype(o_ref.dtype)


def _inproj1(x2, norm_w, shift, scale, w_bf, conv_w, conv_b, rows_per_batch, tm):
    m, d = x2.shape
    c = w_bf.shape[1] // 4
    tn = _tile(c, 256, LANES)
    nj = c // tn
    tpb = rows_per_batch // tm
    hb = tm // HALO
    nhb = m // HALO
    assert rows_per_batch % tm == 0 and tm % HALO == 0

    def wspec(p):
        return pl.BlockSpec((d, tn), lambda i, j: (0, p * nj + j))

    return pl.pallas_call(
        functools.partial(_inproj1_kernel, tpb),
        grid=(m // tm, nj),
        in_specs=[pl.BlockSpec((tm, d), lambda i, j: (i, 0), pipeline_mode=pl.Buffered(1)),
                  pl.BlockSpec((HALO, d), lambda i, j: (jnp.maximum(i * hb - 1, 0), 0)),
                  pl.BlockSpec((HALO, d), lambda i, j: (jnp.minimum((i + 1) * hb, nhb - 1), 0)),
                  pl.BlockSpec((1, d), lambda i, j: (0, 0)),
                  pl.BlockSpec((None, 1, d), lambda i, j: (i // tpb, 0, 0)),
                  pl.BlockSpec((None, 1, d), lambda i, j: (i // tpb, 0, 0)),
                  wspec(0), wspec(1), wspec(2), wspec(3),
                  pl.BlockSpec((CONV_K, tn), lambda i, j: (0, j)),
                  pl.BlockSpec((1, tn), lambda i, j: (0, j))],
        out_specs=pl.BlockSpec((tm, tn), lambda i, j: (i, j)),
        out_shape=jax.ShapeDtypeStruct((m, c), BF16),
        scratch_shapes=[pltpu.VMEM((tm + 2 * HALO, d), BF16)],
        compiler_params=_params("parallel", "arbitrary"),
        name="inproj1",
    )(x2, x2, x2, norm_w.reshape(1, d), shift, scale, w_bf, w_bf, w_bf, w_bf, conv_w, conv_b.reshape(1, c))


def _rope_tables(n_tok):
    rows = n_tok // GRID_W
    r, col = np.meshgrid(np.arange(rows, dtype=np.float64), np.arange(GRID_W, dtype=np.float64), indexing="ij")
    axis_dim = HEAD_DIM // 2
    inv_freq = ROPE_THETA ** (-np.arange(0, axis_dim, 2, dtype=np.float64) / axis_dim)
    ar = r.reshape(-1)[:, None] * inv_freq
    ac = col.reshape(-1)[:, None] * inv_freq
    cos_t = np.concatenate([np.cos(ar), np.cos(ar), np.cos(ac), np.cos(ac)], axis=1)
    sin_t = np.concatenate([-np.sin(ar), np.sin(ar), -np.sin(ac), np.sin(ac)], axis=1)
    return jnp.asarray(cos_t, F32), jnp.asarray(sin_t, F32)


def kernel(x, c, ctx, c_ctx, l0_norm_w, l0_w_mod, l0_b_mod, l0_w_in, l0_q_norm_w, l0_k_norm_w, l0_fwd_lam_re, l0_fwd_lam_im, l0_fwd_log_dt, l0_fwd_b_re, l0_fwd_b_im, l0_fwd_c_re, l0_fwd_c_im, l0_bwd_lam_re, l0_bwd_lam_im, l0_bwd_log_dt, l0_bwd_b_re, l0_bwd_b_im, l0_bwd_c_re, l0_bwd_c_im, l0_ssm_d, l0_w_glu, l0_b_glu, l0_w_out, l1_norm_w, l1_w_mod, l1_b_mod, l1_w_in, l1_conv_w, l1_conv_b, l1_w_out):
    b, l, d = x.shape
    lc = ctx.shape[1]
    ssm_w = l0_ssm_d.shape[0]
    att_w = l0_w_out.shape[0] - ssm_w
    kv_w = att_w // Q_PER_KV
    widths = (att_w, kv_w, kv_w, att_w, ssm_w, ssm_w)
    assert sum(widths) == l0_w_in.shape[1]

    pad = (-(b + 1)) % 16
    cond = jnp.concatenate([c, c_ctx[None, :], jnp.zeros((pad, d), F32)], axis=0)
    mod0 = _ada_mod(cond, l0_w_mod, l0_b_mod)
    mod1 = _ada_mod(cond, l1_w_mod, l1_b_mod)
    shift0, scale0, gate0 = (mod0[:b, k * d:(k + 1) * d].reshape(b, 1, d) for k in range(3))
    shift0c, scale0c = (mod0[b, k * d:(k + 1) * d].reshape(1, d) for k in range(2))
    shift1, scale1, gate1 = (mod1[:b, k * d:(k + 1) * d].reshape(b, 1, d) for k in range(3))

    x2 = x.reshape(b * l, d)
    ctx2 = ctx.reshape(b * lc, d)
    w_in0 = l0_w_in.astype(BF16)

    cos_t, sin_t = _rope_tables(l)
    tm = _tile(l, 512, SUBLANES)
    q, k, v, g_att, u, g_ssm = _inproj0(x2, l0_norm_w, shift0, scale0, w_in0, cos_t, sin_t,
                                        l0_q_norm_w, l0_k_norm_w, widths, l, tm)
    kc, vc, uc = _ctxproj(ctx2, l0_norm_w, shift0c, scale0c, w_in0, l0_k_norm_w, widths)

    k_all = jnp.concatenate([kc.reshape(b, lc, kv_w), k.reshape(b, l, kv_w)], axis=1)
    v_all = jnp.concatenate([vc.reshape(b, lc, kv_w), v.reshape(b, l, kv_w)], axis=1)
    score_bound = HEAD_DIM ** 0.5 * jnp.max(jnp.abs(l0_q_norm_w)) * jnp.max(jnp.abs(l0_k_norm_w))
    att = lax.cond(score_bound < SCORE_BOUND_NO_SHIFT,
                   functools.partial(_attention_noshift, tq=_tile(l, 512, SUBLANES)),
                   functools.partial(_attention, tq=_tile(l, 512, SUBLANES)),
                   q.reshape(b, l, att_w), k_all, v_all, g_att.reshape(b, l, att_w))

    fwd = (l0_fwd_lam_re, l0_fwd_lam_im, l0_fwd_log_dt, l0_fwd_b_re, l0_fwd_b_im, l0_fwd_c_re, l0_fwd_c_im)
    bwd = (l0_bwd_lam_re, l0_bwd_lam_im, l0_bwd_log_dt, l0_bwd_b_re, l0_bwd_b_im, l0_bwd_c_re, l0_bwd_c_im)
    n_in = l0_fwd_b_re.shape[2]
    wf, vf, ktf, af = _s5_params(fwd, False, jnp.zeros((ssm_w // n_in, n_in, n_in), F32))
    wb, vb, ktb, ab = _s5_params(bwd, True, ktf[:, -n_in:, :])
    toep = _toeplitz(ktf, ktb).astype(BF16)
    tmo = _tile(l, 1024, SUBLANES)
    act, w_glu, w_out0 = _s5(u.reshape(b, l, ssm_w), uc.reshape(b, lc, ssm_w), l0_ssm_d, toep,
                             wf, wb, vf, vb, af, ab, side_casts=(l0_w_glu, l0_w_out))
    ssm = _glu(act.reshape(b * l, ssm_w), w_glu, l0_b_glu, g_ssm, tmo)

    assert att_w == ssm_w
    h1, w_in1, w_out1 = _outproj(att.reshape(b * l, att_w), ssm, 0, w_out0, x2, gate0, l, tmo, "outproj0",
                                 side_casts=(l1_w_in, l1_w_out))

    mix1 = _inproj1(h1, l1_norm_w, shift1, scale1, w_in1, l1_conv_w, l1_conv_b, l, tmo)
    h2 = _outproj(mix1, mix1, 1, w_out1, h1, gate1, l, tmo, "outproj1")
    return h2.reshape(b, l, d)
```

```python
import functools
import math

import jax
import jax.numpy as jnp
import numpy as np
from jax import lax
from jax.experimental import pallas as pl
from jax.experimental.pallas import tpu as pltpu

F32 = jnp.float32
BF16 = jnp.bfloat16

EPS = 1e-6
HEAD_DIM = 128
Q_PER_KV = 4
GRID_W = 64
ROPE_THETA = 10000.0
CONV_K = 3
Q_SCALE = HEAD_DIM ** -0.5 * math.log2(math.e)
SCORE_BOUND_NO_SHIFT = 40.0

LANES = 128
SUBLANES = 8
HALO = 16
CAST_ROWS = 16
S5_CHUNK = 16
S5_POW_ROWS = 32
S5_TAB_ROWS = 10 * SUBLANES
VMEM_LIMIT_BYTES = 56 * 1024 * 1024


def _params(*semantics):
    return pltpu.CompilerParams(dimension_semantics=semantics, vmem_limit_bytes=VMEM_LIMIT_BYTES)


def _tile(dim, pref, mult):
    if dim <= pref:
        return dim
    t = (pref // mult) * mult
    while t > mult and dim % t:
        t -= mult
    assert dim % t == 0, (dim, pref, mult)
    return t


def _side_cast_specs(mats, steps, step_of):
    specs = []
    for a in mats:
        slab = CAST_ROWS * -(-a.shape[0] // (CAST_ROWS * steps))
        nblk = a.shape[0] // slab
        assert a.shape[0] % slab == 0 and nblk <= steps, (a.shape, steps)
        specs.append(pl.BlockSpec((slab, a.shape[1]),
                                  lambda *idx, nblk=nblk: (jnp.minimum(step_of(*idx), nblk - 1), 0)))
    return specs


def _side_cast(cast_in, cast_out):
    for src, dst in zip(cast_in, cast_out):
        dst[...] = src[...].astype(BF16)


def _sigmoid(x):
    return 1.0 / (1.0 + jnp.exp(-x))


def _silu(x):
    return x * _sigmoid(x)


def _ada_kernel(c_ref, w_ref, b_ref, o_ref):
    s = _silu(c_ref[...]).astype(BF16)
    o_ref[...] = jnp.dot(s, w_ref[...].astype(BF16), preferred_element_type=F32) + b_ref[...]


def _ada_mod(cond, w_mod, b_mod):
    rows, d = cond.shape
    n = w_mod.shape[1]
    tn = _tile(n, 512, LANES)
    return pl.pallas_call(
        _ada_kernel,
        grid=(n // tn,),
        in_specs=[pl.BlockSpec((rows, d), lambda j: (0, 0)),
                  pl.BlockSpec((d, tn), lambda j: (0, j)),
                  pl.BlockSpec((1, tn), lambda j: (0, j))],
        out_specs=pl.BlockSpec((rows, tn), lambda j: (0, j)),
        out_shape=jax.ShapeDtypeStruct((rows, n), F32),
        compiler_params=_params("parallel"),
        name="ada_mod",
    )(cond, w_mod, b_mod.reshape(1, n))


def _modulated(x, nw, shift, scale):
    ms = jnp.mean(x * x, axis=-1, keepdims=True)
    return (x * lax.rsqrt(ms + EPS) * nw) * (1.0 + scale) + shift


def _norm_rope_heads(acc, nw, cos, sin, out_scale, o_ref):
    lane = lax.broadcasted_iota(jnp.int32, (acc.shape[0], HEAD_DIM), 1)
    first_half = (lane % 64) < 32
    for h in range(acc.shape[1] // HEAD_DIM):
        a = acc[:, h * HEAD_DIM:(h + 1) * HEAD_DIM]
        ms = jnp.mean(a * a, axis=-1, keepdims=True)
        a = a * lax.rsqrt(ms + EPS) * nw
        swapped = jnp.where(first_half, pltpu.roll(a, HEAD_DIM - 32, 1), pltpu.roll(a, 32, 1))
        o_ref[:, h * HEAD_DIM:(h + 1) * HEAD_DIM] = ((a * cos + swapped * sin) * out_scale).astype(o_ref.dtype)


def _inproj0_kernel(bounds, x_ref, nw_ref, sh_ref, sc_ref, w_ref, cos_ref, sin_ref, qn_ref, kn_ref,
                    q_ref, k_ref, v_ref, ga_ref, u_ref, gs_ref, xn_scr, acc_scr):
    j = pl.program_id(1)
    e_q, e_k, e_v, e_ga, e_u = bounds

    def dot():
        return jnp.dot(xn_scr[...], w_ref[...], preferred_element_type=F32)

    @pl.when(j == 0)
    def _():
        xn_scr[...] = _modulated(x_ref[...], nw_ref[...], sh_ref[...], sc_ref[...]).astype(BF16)
        acc_scr[0] = dot()

    @pl.when((j >= 1) & (j <= e_q))
    def _():
        _norm_rope_heads(acc_scr[(j - 1) % 2], qn_ref[...], cos_ref[...], sin_ref[...], Q_SCALE, q_ref)
        acc_scr[j % 2] = dot()

    @pl.when(j == e_k)
    def _():
        _norm_rope_heads(acc_scr[(j - 1) % 2], kn_ref[...], cos_ref[...], sin_ref[...], 1.0, k_ref)
        v_ref[...] = dot().astype(BF16)

    @pl.when((j >= e_v) & (j < e_ga))
    def _():
        ga_ref[...] = _silu(dot()).astype(BF16)

    @pl.when((j >= e_ga) & (j < e_u))
    def _():
        u_ref[...] = dot()

    @pl.when(j >= e_u)
    def _():
        gs_ref[...] = _silu(dot()).astype(BF16)


def _inproj0(x2, norm_w, shift, scale, w_bf, cos_t, sin_t, qn, kn, widths, rows_per_batch, tm):
    m, d = x2.shape
    n = w_bf.shape[1]
    tn = _tile(min(widths), 512, LANES)
    assert all(w % tn == 0 for w in widths) and sum(widths) == n
    ends, off = [], 0
    for w in widths:
        off += w
        ends.append(off // tn)
    starts = [0] + ends[:-1]
    assert ends[1] - starts[1] == 1 and ends[2] - starts[2] == 1, "k and v must be one column tile each"
    tpb = rows_per_batch // tm
    assert rows_per_batch % tm == 0 and m % tm == 0

    def out_spec(p):
        lo, hi = starts[p], ends[p]
        late = 1 if p == 0 else 0
        return pl.BlockSpec((tm, tn), lambda i, j: (i, jnp.clip(j - late - lo, 0, hi - lo - 1)))

    dts = (BF16, BF16, BF16, BF16, F32, BF16)
    return pl.pallas_call(
        functools.partial(_inproj0_kernel, tuple(ends[:5])),
        grid=(m // tm, n // tn),
        in_specs=[pl.BlockSpec((tm, d), lambda i, j: (i, 0)),
                  pl.BlockSpec((1, d), lambda i, j: (0, 0)),
                  pl.BlockSpec((None, 1, d), lambda i, j: (i // tpb, 0, 0)),
                  pl.BlockSpec((None, 1, d), lambda i, j: (i // tpb, 0, 0)),
                  pl.BlockSpec((d, tn), lambda i, j: (0, j)),
                  pl.BlockSpec((tm, HEAD_DIM), lambda i, j: (i % tpb, 0)),
                  pl.BlockSpec((tm, HEAD_DIM), lambda i, j: (i % tpb, 0)),
                  pl.BlockSpec((1, HEAD_DIM), lambda i, j: (0, 0)),
                  pl.BlockSpec((1, HEAD_DIM), lambda i, j: (0, 0))],
        out_specs=[out_spec(p) for p in range(6)],
        out_shape=[jax.ShapeDtypeStruct((m, w), dt) for w, dt in zip(widths, dts)],
        scratch_shapes=[pltpu.VMEM((tm, d), BF16), pltpu.VMEM((2, tm, tn), F32)],
        compiler_params=_params("parallel", "arbitrary"),
        name="inproj0",
    )(x2, norm_w.reshape(1, d), shift, scale, w_bf, cos_t, sin_t, qn.reshape(1, HEAD_DIM), kn.reshape(1, HEAD_DIM))


def _ctxproj_kernel(x_ref, nw_ref, sh_ref, sc_ref, w_ref, kn_ref, k_ref, v_ref, u_ref, xn_scr):
    j = pl.program_id(0)

    @pl.when(j == 0)
    def _():
        xn_scr[...] = _modulated(x_ref[...], nw_ref[...], sh_ref[...], sc_ref[...]).astype(BF16)

    acc = jnp.dot(xn_scr[...], w_ref[...], preferred_element_type=F32)

    @pl.when(j == 0)
    def _():
        for h in range(acc.shape[1] // HEAD_DIM):
            a = acc[:, h * HEAD_DIM:(h + 1) * HEAD_DIM]
            ms = jnp.mean(a * a, axis=-1, keepdims=True)
            k_ref[:, h * HEAD_DIM:(h + 1) * HEAD_DIM] = (a * lax.rsqrt(ms + EPS) * kn_ref[...]).astype(BF16)

    @pl.when(j == 1)
    def _():
        v_ref[...] = acc.astype(BF16)

    @pl.when(j >= 2)
    def _():
        u_ref[...] = acc


def _ctxproj(ctx2, norm_w, shift_c, scale_c, w_bf, kn, widths):
    mc, d = ctx2.shape
    tn = widths[1]
    assert widths[2] == tn and widths[0] % tn == 0 and widths[3] % tn == 0 and widths[4] % tn == 0
    k_tile = widths[0] // tn
    u_tile = (widths[0] + 2 * tn + widths[3]) // tn
    nu = widths[4] // tn
    one = lambda j: (0, 0)
    return pl.pallas_call(
        _ctxproj_kernel,
        grid=(2 + nu,),
        in_specs=[pl.BlockSpec((mc, d), one), pl.BlockSpec((1, d), one), pl.BlockSpec((1, d), one),
                  pl.BlockSpec((1, d), one),
                  pl.BlockSpec((d, tn), lambda j: (0, jnp.where(j < 2, k_tile + j, u_tile + j - 2))),
                  pl.BlockSpec((1, HEAD_DIM), one)],
        out_specs=[pl.BlockSpec((mc, tn), one), pl.BlockSpec((mc, tn), one),
                   pl.BlockSpec((mc, tn), lambda j: (0, jnp.maximum(j - 2, 0)))],
        out_shape=[jax.ShapeDtypeStruct((mc, tn), BF16), jax.ShapeDtypeStruct((mc, tn), BF16),
                   jax.ShapeDtypeStruct((mc, widths[4]), F32)],
        scratch_shapes=[pltpu.VMEM((mc, d), BF16)],
        compiler_params=_params("arbitrary"),
        name="ctxproj",
    )(ctx2, norm_w.reshape(1, d), shift_c, scale_c, w_bf, kn.reshape(1, HEAD_DIM))


def _attn_kernel(n_cast, q_ref, k_ref, v_ref, g_ref, *rest):
    cast_in, o_ref, cast_out = rest[:n_cast], rest[n_cast], rest[n_cast + 1:2 * n_cast + 1]
    m_scr, l_scr, acc_scr = rest[2 * n_cast + 1:]
    _side_cast(cast_in, cast_out)
    kv = pl.program_id(3)

    @pl.when(kv == 0)
    def _():
        m_scr[...] = jnp.full(m_scr.shape, -jnp.inf, F32)
        l_scr[...] = jnp.zeros(l_scr.shape, F32)
        acc_scr[...] = jnp.zeros(acc_scr.shape, F32)

    k = k_ref[...]
    v = v_ref[...]
    for h in range(Q_PER_KV):
        qh = q_ref[:, h * HEAD_DIM:(h + 1) * HEAD_DIM]
        s = lax.dot_general(qh, k, (((1,), (1,)), ((), ())), preferred_element_type=F32)
        m_prev = m_scr[h]
        m_new = jnp.maximum(m_prev, jnp.max(s, axis=1, keepdims=True))
        alpha = jnp.exp2(m_prev - m_new)
        p = jnp.exp2(s - m_new[:, :1])
        l_scr[h] = alpha * l_scr[h] + jnp.sum(p, axis=1, keepdims=True)
        acc_scr[h] = alpha * acc_scr[h] + jnp.dot(p.astype(BF16), v, preferred_element_type=F32)
        m_scr[h] = m_new

    @pl.when(kv == pl.num_programs(3) - 1)
    def _():
        for h in range(Q_PER_KV):
            sl = slice(h * HEAD_DIM, (h + 1) * HEAD_DIM)
            o = acc_scr[h] / l_scr[h]
            o_ref[:, sl] = (o * g_ref[:, sl].astype(F32)).astype(o_ref.dtype)


def _attn_noshift_kernel(n_cast, q_ref, k_ref, v_ref, g_ref, *rest):
    cast_in, o_ref, cast_out = rest[:n_cast], rest[n_cast], rest[n_cast + 1:]
    _side_cast(cast_in, cast_out)
    tq = q_ref.shape[0]
    q = jnp.concatenate([q_ref[:, h * HEAD_DIM:(h + 1) * HEAD_DIM] for h in range(Q_PER_KV)], axis=0)
    s = lax.dot_general(q, k_ref[...], (((1,), (1,)), ((), ())), preferred_element_type=F32)
    p = jnp.exp2(s).astype(BF16)
    v = v_ref[...]
    v_ext = jnp.concatenate([v, jnp.ones_like(v)], axis=1)
    a = jnp.dot(p, v_ext, preferred_element_type=F32)
    for h in range(Q_PER_KV):
        sl = slice(h * HEAD_DIM, (h + 1) * HEAD_DIM)
        ah = a[h * tq:(h + 1) * tq]
        o = ah[:, :HEAD_DIM] / ah[:, HEAD_DIM:]
        o_ref[:, sl] = (o * g_ref[:, sl].astype(F32)).astype(o_ref.dtype)


def _attention_noshift(q, k_all, v_all, gate, *side_casts, tq):
    b, l, aw = q.shape
    s_len, kvw = k_all.shape[1], k_all.shape[2]
    kvh = kvw // HEAD_DIM
    gw = Q_PER_KV * HEAD_DIM
    nq = l // tq
    assert aw == kvh * gw
    cast_specs = _side_cast_specs(side_casts, b * kvh * nq, lambda bi, h, qi: (bi * kvh + h) * nq + qi)
    return pl.pallas_call(
        functools.partial(_attn_noshift_kernel, len(side_casts)),
        grid=(b, kvh, nq),
        in_specs=[pl.BlockSpec((None, tq, gw), lambda bi, h, qi: (bi, qi, h)),
                  pl.BlockSpec((None, s_len, HEAD_DIM), lambda bi, h, qi: (bi, 0, h)),
                  pl.BlockSpec((None, s_len, HEAD_DIM), lambda bi, h, qi: (bi, 0, h)),
                  pl.BlockSpec((None, tq, gw), lambda bi, h, qi: (bi, qi, h))] + cast_specs,
        out_specs=[pl.BlockSpec((None, tq, gw), lambda bi, h, qi: (bi, qi, h))] + cast_specs,
        out_shape=[jax.ShapeDtypeStruct((b, l, aw), BF16)] + [jax.ShapeDtypeStruct(a.shape, BF16) for a in side_casts],
        compiler_params=_params("arbitrary", "arbitrary", "arbitrary"),
        name="attention_noshift",
    )(q, k_all, v_all, gate, *side_casts)


def _attention(q, k_all, v_all, gate, *side_casts, tq):
    tk = _tile(k_all.shape[1], 1408, LANES)
    b, l, aw = q.shape
    s_len, kvw = k_all.shape[1], k_all.shape[2]
    kvh = kvw // HEAD_DIM
    gw = Q_PER_KV * HEAD_DIM
    nq, nk = l // tq, s_len // tk
    assert aw == kvh * gw
    cast_specs = _side_cast_specs(side_casts, b * kvh * nq * nk,
                                  lambda bi, h, qi, ki: ((bi * kvh + h) * nq + qi) * nk + ki)
    return pl.pallas_call(
        functools.partial(_attn_kernel, len(side_casts)),
        grid=(b, kvh, nq, nk),
        in_specs=[pl.BlockSpec((None, tq, gw), lambda bi, h, qi, ki: (bi, qi, h)),
                  pl.BlockSpec((None, tk, HEAD_DIM), lambda bi, h, qi, ki: (bi, ki, h)),
                  pl.BlockSpec((None, tk, HEAD_DIM), lambda bi, h, qi, ki: (bi, ki, h)),
                  pl.BlockSpec((None, tq, gw), lambda bi, h, qi, ki: (bi, qi, h))] + cast_specs,
        out_specs=[pl.BlockSpec((None, tq, gw), lambda bi, h, qi, ki: (bi, qi, h))] + cast_specs,
        out_shape=[jax.ShapeDtypeStruct((b, l, aw), BF16)] + [jax.ShapeDtypeStruct(a.shape, BF16) for a in side_casts],
        scratch_shapes=[pltpu.VMEM((Q_PER_KV, tq, HEAD_DIM), F32)] * 3,
        compiler_params=_params("arbitrary", "arbitrary", "arbitrary", "arbitrary"),
        name="attention",
    )(q, k_all, v_all, gate, *side_casts)


def _s5_param_kernel(reverse, n_cast, lr_ref, li_ref, ldt_ref, b1_ref, bsw_ref, c1_ref, csw_ref, k0_ref, *rest):
    cast_in = rest[:n_cast]
    w_ref, vt_ref, kt_ref, a_ref = rest[n_cast:n_cast + 4]
    cast_out = rest[n_cast + 4:]
    _side_cast(cast_in, cast_out)
    for gi in range(lr_ref.shape[0]):
        _s5_group_params(reverse, lr_ref.at[gi], li_ref.at[gi], ldt_ref.at[gi], b1_ref.at[gi], bsw_ref.at[gi],
                         c1_ref.at[gi], csw_ref.at[gi], k0_ref.at[gi],
                         w_ref.at[gi], vt_ref.at[gi], kt_ref.at[gi], a_ref.at[gi])


def _s5_group_params(reverse, lr_ref, li_ref, ldt_ref, b1_ref, bsw_ref, c1_ref, csw_ref, k0_ref,
                     w_ref, vt_ref, kt_ref, a_ref):
    t = S5_CHUNK
    lr, li = lr_ref[...], li_ref[...]
    dt = jnp.exp(ldt_ref[...])
    lane = lax.broadcasted_iota(jnp.int32, (1, LANES), 1)
    neg_first = jnp.where(lane < 64, -1.0, 1.0).astype(F32)
    jj = lax.broadcasted_iota(jnp.int32, (S5_POW_ROWS, LANES), 0).astype(F32)
    mag = jnp.exp(lr * dt * jj)
    ang = li * dt * jj
    p_re = mag * jnp.cos(ang)
    p_im = mag * jnp.sin(ang)

    a_re, a_im = p_re[1:2], p_im[1:2]
    den = lr * lr + li * li
    n_re = a_re - 1.0
    coef_re = (n_re * lr + a_im * li) / den
    coef_im = (a_im * lr - n_re * li) / den
    b1 = b1_ref[...]
    b2 = bsw_ref[...] * neg_first
    bb1 = coef_re * b1 + coef_im * b2
    bb2 = coef_re * b2 - coef_im * b1

    c1 = c1_ref[...] * (-neg_first)
    csw = csw_ref[...]

    xs = [None] * t
    for j in range(t):
        x_j = p_re[j:j + 1] * bb1 + p_im[j:j + 1] * bb2
        blk = (t - 1 - j) if not reverse else j
        xs[blk] = x_j
        w_ref[blk * 16:(blk + 1) * 16, 0:LANES] = x_j.astype(w_ref.dtype)
        w_ref[blk * 16:(blk + 1) * 16, LANES:2 * LANES] = pltpu.roll(x_j, LANES // 2, 1).astype(w_ref.dtype)
        m = (j + 1) if not reverse else (t - j)
        vt_ref[j * 16:(j + 1) * 16, :] = (p_re[m:m + 1] * c1 - p_im[m:m + 1] * csw).astype(vt_ref.dtype)
    x_all = jnp.concatenate(xs, axis=0)
    kt = lax.dot_general(x_all, c1, (((1,), (1,)), ((), ())),
                         precision=lax.Precision.HIGHEST, preferred_element_type=F32)
    kt_ref[...] = kt
    lag0 = 0 if reverse else (t - 1) * 16
    kt_ref[lag0:lag0 + 16, :] = kt[lag0:lag0 + 16] + k0_ref[...]

    r = lax.broadcasted_iota(jnp.int32, (SUBLANES, LANES), 0)
    last = SUBLANES - 1

    def table(k, expo, keep):
        e = expo.astype(F32) * float(t)
        mg = jnp.exp(lr * dt * e)
        an = li * dt * e
        a_ref[2 * k * SUBLANES:(2 * k + 1) * SUBLANES, :] = jnp.where(keep, mg * jnp.cos(an), 0.0)
        a_ref[(2 * k + 1) * SUBLANES:(2 * k + 2) * SUBLANES, :] = jnp.where(keep, mg * jnp.sin(an) * neg_first, 0.0)

    for k, d in enumerate((1, 2, 4)):
        table(k, jnp.full_like(r, d), (r <= last - d) if reverse else (r >= d))
    table(3, (last - r) if reverse else r, r >= 0)
    table(4, jnp.full_like(r, SUBLANES), r >= 0)


def _s5_params(prm, reverse, k0, side_casts=()):
    lam_re, lam_im, log_dt, b_re, b_im, c_re, c_im = prm
    g, p = lam_re.shape
    i = b_re.shape[2]
    assert 2 * p == LANES and i * S5_CHUNK == 256

    def dup(a):
        return jnp.concatenate([a, a], axis=-1).reshape(g, 1, LANES).astype(F32)

    bt_re, bt_im = jnp.swapaxes(b_re, 1, 2).astype(F32), jnp.swapaxes(b_im, 1, 2).astype(F32)
    b1 = jnp.concatenate([bt_re, bt_im], axis=-1)
    bsw = jnp.concatenate([bt_im, bt_re], axis=-1)
    c1 = jnp.concatenate([c_re, c_im], axis=-1).astype(F32)
    csw = jnp.concatenate([c_im, c_re], axis=-1).astype(F32)
    ldt = jnp.broadcast_to(log_dt.astype(F32)[:, None, None], (g, 1, LANES))
    gb = 8 if g % 8 == 0 else 1

    def spec(rows, cols):
        return pl.BlockSpec((gb, rows, cols), lambda gi: (gi, 0, 0))

    cast_specs = _side_cast_specs(side_casts, g // gb, lambda gi: gi)
    return pl.pallas_call(
        functools.partial(_s5_param_kernel, reverse, len(side_casts)),
        grid=(g // gb,),
        in_specs=[spec(1, LANES)] * 3 + [spec(i, LANES)] * 4 + [spec(i, i)] + cast_specs,
        out_specs=[spec(256, 2 * LANES), spec(256, LANES), spec(256, i), spec(S5_TAB_ROWS, LANES)] + cast_specs,
        out_shape=[jax.ShapeDtypeStruct((g, 256, 2 * LANES), BF16),
                   jax.ShapeDtypeStruct((g, 256, LANES), BF16),
                   jax.ShapeDtypeStruct((g, 256, i), F32),
                   jax.ShapeDtypeStruct((g, S5_TAB_ROWS, LANES), F32)]
        + [jax.ShapeDtypeStruct(a.shape, BF16) for a in side_casts],
        compiler_params=_params("arbitrary"),
        name="s5_params",
    )(dup(lam_re), dup(lam_im), ldt, b1, bsw, c1, csw, k0, *side_casts)


def _toeplitz(kt_f, kt_b):
    t = S5_CHUNK
    g, rows, i = kt_f.shape
    z = jnp.concatenate([kt_f[:, :rows - i], kt_b], axis=1)
    cols = [z[:, (t - 1 - tt) * i:(t - 1 - tt) * i + rows] for tt in range(t)]
    return jnp.stack(cols, axis=2).reshape(g, rows, t * i)


def _slot_transpose8(vs, slot):
    vs = list(vs)
    for d in (4, 2, 1):
        keep = (slot & d) == 0
        new = list(vs)
        for r in range(8):
            if r & d == 0:
                a, b = vs[r], vs[r + d]
                new[r] = jnp.where(keep, a, pltpu.roll(b, 16 * d, 1))
                new[r + d] = jnp.where(keep, pltpu.roll(a, LANES - 16 * d, 1), b)
        vs = new
    return vs


def _gather_chunks(src_ref, dst_scr, n, rb):
    t = S5_CHUNK
    for r0 in range(0, n, rb):
        slot = lax.broadcasted_iota(jnp.int32, (rb, LANES), 1) // 16
        for hi in range(2):
            pieces = [src_ref[pl.ds(r0 * t + hi * 8 + r, rb, stride=t), :] for r in range(8)]
            outs = _slot_transpose8(pieces, slot)
            for gi in range(8):
                dst_scr[gi, r0:r0 + rb, hi * LANES:(hi + 1) * LANES] = outs[gi].astype(BF16)


def _s5_kernel(n, nc, rb, n_cast, ux_ref, uc_ref, d_ref, toep_ref, wf_ref, wb_ref, vf_ref, vb_ref, af_ref, ab_ref,
               *rest):
    cast_in, y_ref, cast_out = rest[:n_cast], rest[n_cast], rest[n_cast + 1:2 * n_cast + 1]
    u_scr, ucx_scr, wf_scr, wfs_scr, wb_scr, wbs_scr, hf_scr, gb_scr = rest[2 * n_cast + 1:]
    _side_cast(cast_in, cast_out)
    t = S5_CHUNK
    ntot = n + nc
    nblk = ntot // SUBLANES
    width = 8 * LANES
    _gather_chunks(ux_ref, u_scr, n, rb)
    _gather_chunks(uc_ref, ucx_scr, nc, min(rb, nc))

    for gi in range(8):
        ls = slice(gi * LANES, (gi + 1) * LANES)
        ug, ucg = u_scr[gi], ucx_scr[gi]
        for w_ref, x_rows, c_rows, dst, dst_s in ((wf_ref, slice(nc, ntot), slice(0, nc), wf_scr, wfs_scr),
                                                  (wb_ref, slice(0, n), slice(n, ntot), wb_scr, wbs_scr)):
            w_x = jnp.dot(ug, w_ref[gi], preferred_element_type=F32)
            w_c = jnp.dot(ucg, w_ref[gi], preferred_element_type=F32)
            dst[x_rows, ls], dst_s[x_rows, ls] = w_x[:, :LANES], w_x[:, LANES:]
            dst[c_rows, ls], dst_s[c_rows, ls] = w_c[:, :LANES], w_c[:, LANES:]

    def tables(a_ref):
        return [jnp.concatenate([a_ref[gi, k * SUBLANES:(k + 1) * SUBLANES, :] for gi in range(8)], axis=1)
                for k in range(S5_TAB_ROWS // SUBLANES)]

    tab_f, tab_b = tables(af_ref), tables(ab_ref)
    row = lax.broadcasted_iota(jnp.int32, (SUBLANES, width), 0)

    def block(src, src_s, dst, r0, h, hs, tab, reverse):
        x, xs = src[pl.ds(r0, SUBLANES), :], src_s[pl.ds(r0, SUBLANES), :]
        for k, d in enumerate((1, 2, 4)):
            sh = SUBLANES - d if reverse else d
            sx, sxs = pltpu.roll(x, sh, 0), pltpu.roll(xs, sh, 0)
            x, xs = x + tab[2 * k] * sx + tab[2 * k + 1] * sxs, xs + tab[2 * k] * sxs - tab[2 * k + 1] * sx
        before = pltpu.roll(x, SUBLANES - 1 if reverse else 1, 0)
        before = jnp.where(row == (SUBLANES - 1 if reverse else 0), 0.0, before)
        hb, hsb = jnp.broadcast_to(h, x.shape), jnp.broadcast_to(hs, x.shape)
        dst[pl.ds(r0, SUBLANES), :] = before + tab[6] * hb + tab[7] * hsb
        e = 0 if reverse else SUBLANES - 1
        q1, q2 = tab[8][0:1], tab[9][0:1]
        return x[e:e + 1] + q1 * h + q2 * hs, xs[e:e + 1] + q1 * hs - q2 * h

    def step(i, carry):
        h, hs, g, gs = carry
        h, hs = block(wf_scr, wfs_scr, hf_scr, pl.multiple_of(i * SUBLANES, SUBLANES), h, hs, tab_f, False)
        g, gs = block(wb_scr, wbs_scr, gb_scr, pl.multiple_of((nblk - 1 - i) * SUBLANES, SUBLANES), g, gs,
                      tab_b, True)
        return h, hs, g, gs

    zero = jnp.zeros((1, width), F32)
    lax.fori_loop(0, nblk, step, (zero, zero, zero, zero))

    nt = (((1,), (1,)), ((), ()))
    for r0 in range(0, n, rb):
        slot = lax.broadcasted_iota(jnp.int32, (rb, LANES), 1) // 16
        ys = []
        for gi in range(8):
            ls = slice(gi * LANES, (gi + 1) * LANES)
            y = jnp.dot(u_scr[gi, r0:r0 + rb, :], toep_ref[gi], preferred_element_type=F32)
            y += lax.dot_general(hf_scr[nc + r0:nc + r0 + rb, ls].astype(BF16), vf_ref[gi], nt,
                                 preferred_element_type=F32)
            y += lax.dot_general(gb_scr[r0:r0 + rb, ls].astype(BF16), vb_ref[gi], nt,
                                 preferred_element_type=F32)
            ys.append(y)
        for hi in range(2):
            outs = _slot_transpose8([ys[gi][:, hi * LANES:(hi + 1) * LANES] for gi in range(8)], slot)
            for r in range(8):
                rows = pl.ds(r0 * t + hi * 8 + r, rb, stride=t)
                y_ref[rows, :] = jax.nn.gelu(outs[r] + d_ref[...] * ux_ref[rows, :])


def _s5(u, uc, d_skip, toep, wf2, wb2, vf, vb, af, ab, side_casts=()):
    b, l, c = u.shape
    lc = uc.shape[1]
    t = S5_CHUNK
    n, nc = l // t, lc // t
    assert l % t == 0 and lc % t == 0 and c % LANES == 0 and n % SUBLANES == 0 and nc % SUBLANES == 0
    rb = _tile(n, 128, SUBLANES)
    ntot = n + nc
    gmat = lambda rows, last: pl.BlockSpec((8, rows, last), lambda bi, ci: (ci, 0, 0))
    nci = c // LANES
    cast_specs = _side_cast_specs(side_casts, b * nci, lambda bi, ci: bi * nci + ci)
    out = pl.pallas_call(
        functools.partial(_s5_kernel, n, nc, rb, len(side_casts)),
        grid=(b, nci),
        in_specs=[pl.BlockSpec((None, l, LANES), lambda bi, ci: (bi, 0, ci)),
                  pl.BlockSpec((None, lc, LANES), lambda bi, ci: (bi, 0, ci)),
                  pl.BlockSpec((1, LANES), lambda bi, ci: (0, ci)),
                  gmat(256, 256), gmat(256, 256), gmat(256, 256), gmat(256, LANES), gmat(256, LANES),
                  gmat(S5_TAB_ROWS, LANES), gmat(S5_TAB_ROWS, LANES)] + cast_specs,
        out_specs=[pl.BlockSpec((None, l, LANES), lambda bi, ci: (bi, 0, ci))] + cast_specs,
        out_shape=[jax.ShapeDtypeStruct((b, l, c), F32)] + [jax.ShapeDtypeStruct(a.shape, BF16) for a in side_casts],
        scratch_shapes=[pltpu.VMEM((8, n, 256), BF16),
                        pltpu.VMEM((8, nc, 256), BF16)] + [pltpu.VMEM((ntot, 8 * LANES), F32)] * 6,
        compiler_params=_params("arbitrary", "arbitrary"),
        name="s5_scan",
    )(u, uc, d_skip.reshape(1, c), toep, wf2, wb2, vf, vb, af, ab, *side_casts)
    return out if side_casts else out[0]


def _glu_kernel(v_ref, vj_ref, w_ref, b_ref, g_ref, o_ref, vb_scr):
    @pl.when(pl.program_id(1) == 0)
    def _():
        vb_scr[...] = v_ref[...].astype(BF16)

    acc = jnp.dot(vb_scr[...], w_ref[...], preferred_element_type=F32) + b_ref[...]
    o_ref[...] = (vj_ref[...] * _sigmoid(acc) * g_ref[...].astype(F32)).astype(o_ref.dtype)


def _glu(v, w_bf, b_glu, gate, tm):
    m, c = v.shape
    tn = _tile(c, 1024, LANES)
    return pl.pallas_call(
        _glu_kernel,
        grid=(m // tm, c // tn),
        in_specs=[pl.BlockSpec((tm, c), lambda i, j: (i, 0)),
                  pl.BlockSpec((tm, tn), lambda i, j: (i, j)),
                  pl.BlockSpec((c, tn), lambda i, j: (0, j)),
                  pl.BlockSpec((1, tn), lambda i, j: (0, j)),
                  pl.BlockSpec((tm, tn), lambda i, j: (i, j))],
        out_specs=pl.BlockSpec((tm, tn), lambda i, j: (i, j)),
        out_shape=jax.ShapeDtypeStruct((m, c), BF16),
        scratch_shapes=[pltpu.VMEM((tm, c), BF16)],
        compiler_params=_params("parallel", "arbitrary"),
        name="ssm_glu",
    )(v, v, w_bf, b_glu.reshape(1, c), gate)


def _outproj_kernel(n_cast, a_ref, s_ref, wa_ref, ws_ref, x_ref, g_ref, *rest):
    cast_in, o_ref, cast_out = rest[:n_cast], rest[n_cast], rest[n_cast + 1:]
    _side_cast(cast_in, cast_out)
    acc = jnp.dot(a_ref[...], wa_ref[...], preferred_element_type=F32)
    acc += jnp.dot(s_ref[...], ws_ref[...], preferred_element_type=F32)
    o_ref[...] = x_ref[...] + g_ref[...] * acc


def _outproj(lhs_a, lhs_s, col_s, w_bf, x2, gate, rows_per_batch, tm, name, side_casts=()):
    m = lhs_a.shape[0]
    kh = w_bf.shape[0] // 2
    d = w_bf.shape[1]
    tn = _tile(d, 512 if side_casts else 1024, LANES)
    nj = d // tn
    tpb = rows_per_batch // tm
    assert rows_per_batch % tm == 0 and lhs_a.shape[1] % kh == 0 and lhs_s.shape[1] % kh == 0
    cast_specs = _side_cast_specs(side_casts, (m // tm) * nj, lambda i, j: i * nj + j)
    out = pl.pallas_call(
        functools.partial(_outproj_kernel, len(side_casts)),
        grid=(m // tm, nj),
        in_specs=[pl.BlockSpec((tm, kh), lambda i, j: (i, 0)),
                  pl.BlockSpec((tm, kh), lambda i, j: (i, col_s)),
                  pl.BlockSpec((kh, tn), lambda i, j: (0, j)),
                  pl.BlockSpec((kh, tn), lambda i, j: (1, j)),
                  pl.BlockSpec((tm, tn), lambda i, j: (i, j)),
                  pl.BlockSpec((None, 1, tn), lambda i, j: (i // tpb, 0, j))] + cast_specs,
        out_specs=[pl.BlockSpec((tm, tn), lambda i, j: (i, j))] + cast_specs,
        out_shape=[jax.ShapeDtypeStruct((m, d), F32)] + [jax.ShapeDtypeStruct(a.shape, BF16) for a in side_casts],
        compiler_params=_params("arbitrary", "arbitrary"),
        name=name,
    )(lhs_a, lhs_s, w_bf, w_bf, x2, gate, *side_casts)
    return out if side_casts else out[0]


def _inproj1_kernel(tpb, x_ref, xp_ref, xq_ref, nw_ref, sh_ref, sc_ref, wb_ref, wc_ref, wx_ref, wg_ref,
                    cw_ref, cb_ref, o_ref, xn_scr):
    i = pl.program_id(0)
    tm = x_ref.shape[0]

    @pl.when(pl.program_id(1) == 0)
    def _():
        nw, sh, sc = nw_ref[...], sh_ref[...], sc_ref[...]
        xn_scr[0:HALO, :] = _modulated(xp_ref[...], nw, sh, sc).astype(BF16)
        slab = min(tm, 256)
        for r in range(0, tm, slab):
            xn_scr[HALO + r:HALO + r + slab, :] = _modulated(x_ref[r:r + slab, :], nw, sh, sc).astype(BF16)
        xn_scr[HALO + tm:, :] = _modulated(xq_ref[...], nw, sh, sc).astype(BF16)

    xn_all = xn_scr[...]
    xn = xn_scr[HALO:HALO + tm, :]
    c_gate = jnp.dot(xn_all, wc_ref[...], preferred_element_type=F32)
    xin = jnp.dot(xn_all, wx_ref[...], preferred_element_type=F32)
    b_gate = jnp.dot(xn, wb_ref[...], preferred_element_type=F32)
    g = jnp.dot(xn, wg_ref[...], preferred_element_type=F32)

    y = c_gate * xin
    rows = y.shape[0]
    row = lax.broadcasted_iota(jnp.int32, y.shape, 0)
    lo = jnp.where((i % tpb) == 0, HALO, 0)
    hi = jnp.where((i % tpb) == tpb - 1, HALO + tm, rows)
    y = jnp.where((row >= lo) & (row < hi), y, 0.0)
    y_prev = pltpu.roll(y, 1, 0)[HALO:HALO + tm]
    y_next = pltpu.roll(y, rows - 1, 0)[HALO:HALO + tm]
    conv = cb_ref[...] + y_prev * cw_ref[0:1, :] + y[HALO:HALO + tm] * cw_ref[1:2, :] + y_next * cw_ref[2:3, :]
    o_ref[...] = (b_gate * conv * _silu(g)).astype(o_ref.dtype)


def _inproj1(x2, norm_w, shift, scale, w_bf, conv_w, conv_b, rows_per_batch, tm):
    m, d = x2.shape
    c = w_bf.shape[1] // 4
    tn = _tile(c, 256, LANES)
    nj = c // tn
    tpb = rows_per_batch // tm
    hb = tm // HALO
    nhb = m // HALO
    assert rows_per_batch % tm == 0 and tm % HALO == 0

    def wspec(p):
        return pl.BlockSpec((d, tn), lambda i, j: (0, p * nj + j))

    return pl.pallas_call(
        functools.partial(_inproj1_kernel, tpb),
        grid=(m // tm, nj),
        in_specs=[pl.BlockSpec((tm, d), lambda i, j: (i, 0), pipeline_mode=pl.Buffered(1)),
                  pl.BlockSpec((HALO, d), lambda i, j: (jnp.maximum(i * hb - 1, 0), 0)),
                  pl.BlockSpec((HALO, d), lambda i, j: (jnp.minimum((i + 1) * hb, nhb - 1), 0)),
                  pl.BlockSpec((1, d), lambda i, j: (0, 0)),
                  pl.BlockSpec((None, 1, d), lambda i, j: (i // tpb, 0, 0)),
                  pl.BlockSpec((None, 1, d), lambda i, j: (i // tpb, 0, 0)),
                  wspec(0), wspec(1), wspec(2), wspec(3),
                  pl.BlockSpec((CONV_K, tn), lambda i, j: (0, j)),
                  pl.BlockSpec((1, tn), lambda i, j: (0, j))],
        out_specs=pl.BlockSpec((tm, tn), lambda i, j: (i, j)),
        out_shape=jax.ShapeDtypeStruct((m, c), BF16),
        scratch_shapes=[pltpu.VMEM((tm + 2 * HALO, d), BF16)],
        compiler_params=_params("parallel", "arbitrary"),
        name="inproj1",
    )(x2, x2, x2, norm_w.reshape(1, d), shift, scale, w_bf, w_bf, w_bf, w_bf, conv_w, conv_b.reshape(1, c))


def _rope_tables(n_tok):
    rows = n_tok // GRID_W
    r, col = np.meshgrid(np.arange(rows, dtype=np.float64), np.arange(GRID_W, dtype=np.float64), indexing="ij")
    axis_dim = HEAD_DIM // 2
    inv_freq = ROPE_THETA ** (-np.arange(0, axis_dim, 2, dtype=np.float64) / axis_dim)
    ar = r.reshape(-1)[:, None] * inv_freq
    ac = col.reshape(-1)[:, None] * inv_freq
    cos_t = np.concatenate([np.cos(ar), np.cos(ar), np.cos(ac), np.cos(ac)], axis=1)
    sin_t = np.concatenate([-np.sin(ar), np.sin(ar), -np.sin(ac), np.sin(ac)], axis=1)
    return jnp.asarray(cos_t, F32), jnp.asarray(sin_t, F32)


def kernel(x, c, ctx, c_ctx, l0_norm_w, l0_w_mod, l0_b_mod, l0_w_in, l0_q_norm_w, l0_k_norm_w, l0_fwd_lam_re, l0_fwd_lam_im, l0_fwd_log_dt, l0_fwd_b_re, l0_fwd_b_im, l0_fwd_c_re, l0_fwd_c_im, l0_bwd_lam_re, l0_bwd_lam_im, l0_bwd_log_dt, l0_bwd_b_re, l0_bwd_b_im, l0_bwd_c_re, l0_bwd_c_im, l0_ssm_d, l0_w_glu, l0_b_glu, l0_w_out, l1_norm_w, l1_w_mod, l1_b_mod, l1_w_in, l1_conv_w, l1_conv_b, l1_w_out):
    b, l, d = x.shape
    lc = ctx.shape[1]
    ssm_w = l0_ssm_d.shape[0]
    att_w = l0_w_out.shape[0] - ssm_w
    kv_w = att_w // Q_PER_KV
    widths = (att_w, kv_w, kv_w, att_w, ssm_w, ssm_w)
    assert sum(widths) == l0_w_in.shape[1]

    pad = (-(b + 1)) % 16
    cond = jnp.concatenate([c, c_ctx[None, :], jnp.zeros((pad, d), F32)], axis=0)
    mod0 = _ada_mod(cond, l0_w_mod, l0_b_mod)
    mod1 = _ada_mod(cond, l1_w_mod, l1_b_mod)
    shift0, scale0, gate0 = (mod0[:b, k * d:(k + 1) * d].reshape(b, 1, d) for k in range(3))
    shift0c, scale0c = (mod0[b, k * d:(k + 1) * d].reshape(1, d) for k in range(2))
    shift1, scale1, gate1 = (mod1[:b, k * d:(k + 1) * d].reshape(b, 1, d) for k in range(3))

    x2 = x.reshape(b * l, d)
    ctx2 = ctx.reshape(b * lc, d)
    n_in = l0_fwd_b_re.shape[2]
    fwd = (l0_fwd_lam_re, l0_fwd_lam_im, l0_fwd_log_dt, l0_fwd_b_re, l0_fwd_b_im, l0_fwd_c_re, l0_fwd_c_im)
    wf, vf, ktf, af, w_in0 = _s5_params(fwd, False, jnp.zeros((ssm_w // n_in, n_in, n_in), F32),
                                        side_casts=(l0_w_in,))

    cos_t, sin_t = _rope_tables(l)
    tm = _tile(l, 512, SUBLANES)
    q, k, v, g_att, u, g_ssm = _inproj0(x2, l0_norm_w, shift0, scale0, w_in0, cos_t, sin_t,
                                        l0_q_norm_w, l0_k_norm_w, widths, l, tm)
    kc, vc, uc = _ctxproj(ctx2, l0_norm_w, shift0c, scale0c, w_in0, l0_k_norm_w, widths)

    k_all = jnp.concatenate([kc.reshape(b, lc, kv_w), k.reshape(b, l, kv_w)], axis=1)
    v_all = jnp.concatenate([vc.reshape(b, lc, kv_w), v.reshape(b, l, kv_w)], axis=1)
    score_bound = HEAD_DIM ** 0.5 * jnp.max(jnp.abs(l0_q_norm_w)) * jnp.max(jnp.abs(l0_k_norm_w))
    att, w_in1, w_out1 = lax.cond(score_bound < SCORE_BOUND_NO_SHIFT,
                                  functools.partial(_attention_noshift, tq=_tile(l, 512, SUBLANES)),
                                  functools.partial(_attention, tq=_tile(l, 512, SUBLANES)),
                                  q.reshape(b, l, att_w), k_all, v_all, g_att.reshape(b, l, att_w),
                                  l1_w_in, l1_w_out)

    bwd = (l0_bwd_lam_re, l0_bwd_lam_im, l0_bwd_log_dt, l0_bwd_b_re, l0_bwd_b_im, l0_bwd_c_re, l0_bwd_c_im)
    wb, vb, ktb, ab = _s5_params(bwd, True, ktf[:, -n_in:, :])
    toep = _toeplitz(ktf, ktb).astype(BF16)
    tmo = _tile(l, 1024, SUBLANES)
    act, w_glu, w_out0 = _s5(u.reshape(b, l, ssm_w), uc.reshape(b, lc, ssm_w), l0_ssm_d, toep,
                             wf, wb, vf, vb, af, ab, side_casts=(l0_w_glu, l0_w_out))
    ssm = _glu(act.reshape(b * l, ssm_w), w_glu, l0_b_glu, g_ssm, tmo)

    assert att_w == ssm_w
    h1 = _outproj(att.reshape(b * l, att_w), ssm, 0, w_out0, x2, gate0, l, tmo, "outproj0")

    mix1 = _inproj1(h1, l1_norm_w, shift1, scale1, w_in1, l1_conv_w, l1_conv_b, l, tmo)
    h2 = _outproj(mix1, mix1, 1, w_out1, h1, gate1, l, tmo, "outproj1")
    return h2.reshape(b, l, d)
```

```python
import functools
import math

import jax
import jax.numpy as jnp
import numpy as np
from jax import lax
from jax.experimental import pallas as pl
from jax.experimental.pallas import tpu as pltpu

F32 = jnp.float32
BF16 = jnp.bfloat16

EPS = 1e-6
HEAD_DIM = 128
Q_PER_KV = 4
GRID_W = 64
ROPE_THETA = 10000.0
CONV_K = 3
Q_SCALE = HEAD_DIM ** -0.5 * math.log2(math.e)
SCORE_BOUND_NO_SHIFT = 40.0

LANES = 128
SUBLANES = 8
HALO = 16
CAST_ROWS = 16
S5_CHUNK = 16
S5_POW_ROWS = 32
S5_TAB_ROWS = 10 * SUBLANES
VMEM_LIMIT_BYTES = 56 * 1024 * 1024


def _params(*semantics):
    return pltpu.CompilerParams(dimension_semantics=semantics, vmem_limit_bytes=VMEM_LIMIT_BYTES)


def _tile(dim, pref, mult):
    if dim <= pref:
        return dim
    t = (pref // mult) * mult
    while t > mult and dim % t:
        t -= mult
    assert dim % t == 0, (dim, pref, mult)
    return t


def _side_cast_specs(mats, steps, step_of):
    specs = []
    for a in mats:
        slab = CAST_ROWS * -(-a.shape[0] // (CAST_ROWS * steps))
        nblk = a.shape[0] // slab
        assert a.shape[0] % slab == 0 and nblk <= steps, (a.shape, steps)
        specs.append(pl.BlockSpec((slab, a.shape[1]),
                                  lambda *idx, nblk=nblk: (jnp.minimum(step_of(*idx), nblk - 1), 0)))
    return specs


def _side_cast(cast_in, cast_out):
    for src, dst in zip(cast_in, cast_out):
        dst[...] = src[...].astype(BF16)


def _sigmoid(x):
    return 1.0 / (1.0 + jnp.exp(-x))


def _silu(x):
    return x * _sigmoid(x)


def _ada_kernel(c_ref, w_ref, b_ref, o_ref):
    s = _silu(c_ref[...]).astype(BF16)
    o_ref[...] = jnp.dot(s, w_ref[...].astype(BF16), preferred_element_type=F32) + b_ref[...]


def _ada_mod(cond, w_mod, b_mod):
    rows, d = cond.shape
    n = w_mod.shape[1]
    tn = _tile(n, 1024, LANES)
    return pl.pallas_call(
        _ada_kernel,
        grid=(n // tn,),
        in_specs=[pl.BlockSpec((rows, d), lambda j: (0, 0)),
                  pl.BlockSpec((d, tn), lambda j: (0, j)),
                  pl.BlockSpec((1, tn), lambda j: (0, j))],
        out_specs=pl.BlockSpec((rows, tn), lambda j: (0, j)),
        out_shape=jax.ShapeDtypeStruct((rows, n), F32),
        compiler_params=_params("parallel"),
        name="ada_mod",
    )(cond, w_mod, b_mod.reshape(1, n))


def _modulated(x, nw, shift, scale):
    ms = jnp.mean(x * x, axis=-1, keepdims=True)
    return (x * lax.rsqrt(ms + EPS)) * (nw * (1.0 + scale)) + shift


def _norm_rope_heads(acc, nw, cos, sin, out_scale, o_ref):
    lane = lax.broadcasted_iota(jnp.int32, (acc.shape[0], HEAD_DIM), 1)
    first_half = (lane % 64) < 32
    for h in range(acc.shape[1] // HEAD_DIM):
        a = acc[:, h * HEAD_DIM:(h + 1) * HEAD_DIM]
        ms = jnp.mean(a * a, axis=-1, keepdims=True)
        a = a * lax.rsqrt(ms + EPS) * nw
        swapped = jnp.where(first_half, pltpu.roll(a, HEAD_DIM - 32, 1), pltpu.roll(a, 32, 1))
        o_ref[:, h * HEAD_DIM:(h + 1) * HEAD_DIM] = ((a * cos + swapped * sin) * out_scale).astype(o_ref.dtype)


def _split_k_dot(xn_scr, wt_ref, wb_ref):
    kh = wt_ref.shape[0]
    acc = jnp.dot(xn_scr[:, :kh], wt_ref[...], preferred_element_type=F32)
    return acc + jnp.dot(xn_scr[:, kh:], wb_ref[...], preferred_element_type=F32)


def _inproj0_kernel(bounds, x_ref, nw_ref, sh_ref, sc_ref, wt_ref, wb_ref, cos_ref, sin_ref, qn_ref, kn_ref,
                    q_ref, k_ref, v_ref, ga_ref, u_ref, gs_ref, xn_scr, acc_scr):
    j = pl.program_id(1)
    e_q, e_k, e_v, e_ga, e_u = bounds

    def dot():
        return _split_k_dot(xn_scr, wt_ref, wb_ref)

    @pl.when(j == 0)
    def _():
        xn_scr[...] = _modulated(x_ref[...], nw_ref[...], sh_ref[...], sc_ref[...]).astype(BF16)
        acc_scr[0] = dot()

    @pl.when((j >= 1) & (j <= e_q))
    def _():
        _norm_rope_heads(acc_scr[(j - 1) % 2], qn_ref[...], cos_ref[...], sin_ref[...], Q_SCALE, q_ref)
        acc_scr[j % 2] = dot()

    @pl.when(j == e_k)
    def _():
        _norm_rope_heads(acc_scr[(j - 1) % 2], kn_ref[...], cos_ref[...], sin_ref[...], 1.0, k_ref)
        v_ref[...] = dot().astype(BF16)

    @pl.when((j >= e_v) & (j < e_ga))
    def _():
        ga_ref[...] = _silu(dot()).astype(BF16)

    @pl.when((j >= e_ga) & (j < e_u))
    def _():
        u_ref[...] = dot()

    @pl.when(j >= e_u)
    def _():
        gs_ref[...] = _silu(dot()).astype(BF16)


def _inproj0(x2, norm_w, shift, scale, w_halves, cos_t, sin_t, qn, kn, widths, rows_per_batch, tm):
    m, d = x2.shape
    w_top, w_bot = w_halves
    n = w_top.shape[1]
    tn = _tile(min(widths), 512, LANES)
    assert all(w % tn == 0 for w in widths) and sum(widths) == n and w_top.shape[0] + w_bot.shape[0] == d
    ends, off = [], 0
    for w in widths:
        off += w
        ends.append(off // tn)
    starts = [0] + ends[:-1]
    assert ends[1] - starts[1] == 1 and ends[2] - starts[2] == 1, "k and v must be one column tile each"
    tpb = rows_per_batch // tm
    assert rows_per_batch % tm == 0 and m % tm == 0

    def out_spec(p):
        lo, hi = starts[p], ends[p]
        late = 1 if p == 0 else 0
        return pl.BlockSpec((tm, tn), lambda i, j: (i, jnp.clip(j - late - lo, 0, hi - lo - 1)))

    dts = (BF16, BF16, BF16, BF16, F32, BF16)
    return pl.pallas_call(
        functools.partial(_inproj0_kernel, tuple(ends[:5])),
        grid=(m // tm, n // tn),
        in_specs=[pl.BlockSpec((tm, d), lambda i, j: (i, 0)),
                  pl.BlockSpec((1, d), lambda i, j: (0, 0)),
                  pl.BlockSpec((None, 1, d), lambda i, j: (i // tpb, 0, 0)),
                  pl.BlockSpec((None, 1, d), lambda i, j: (i // tpb, 0, 0)),
                  pl.BlockSpec((w_top.shape[0], tn), lambda i, j: (0, j)),
                  pl.BlockSpec((w_bot.shape[0], tn), lambda i, j: (0, j)),
                  pl.BlockSpec((tm, HEAD_DIM), lambda i, j: (i % tpb, 0)),
                  pl.BlockSpec((tm, HEAD_DIM), lambda i, j: (i % tpb, 0)),
                  pl.BlockSpec((1, HEAD_DIM), lambda i, j: (0, 0)),
                  pl.BlockSpec((1, HEAD_DIM), lambda i, j: (0, 0))],
        out_specs=[out_spec(p) for p in range(6)],
        out_shape=[jax.ShapeDtypeStruct((m, w), dt) for w, dt in zip(widths, dts)],
        scratch_shapes=[pltpu.VMEM((tm, d), BF16), pltpu.VMEM((2, tm, tn), F32)],
        compiler_params=_params("parallel", "arbitrary"),
        name="inproj0",
    )(x2, norm_w.reshape(1, d), shift, scale, w_top, w_bot, cos_t, sin_t,
      qn.reshape(1, HEAD_DIM), kn.reshape(1, HEAD_DIM))


def _ctxproj_kernel(x_ref, nw_ref, sh_ref, sc_ref, wt_ref, wb_ref, kn_ref, k_ref, v_ref, u_ref, xn_scr):
    j = pl.program_id(0)

    @pl.when(j == 0)
    def _():
        xn_scr[...] = _modulated(x_ref[...], nw_ref[...], sh_ref[...], sc_ref[...]).astype(BF16)

    acc = _split_k_dot(xn_scr, wt_ref, wb_ref)

    @pl.when(j == 0)
    def _():
        for h in range(acc.shape[1] // HEAD_DIM):
            a = acc[:, h * HEAD_DIM:(h + 1) * HEAD_DIM]
            ms = jnp.mean(a * a, axis=-1, keepdims=True)
            k_ref[:, h * HEAD_DIM:(h + 1) * HEAD_DIM] = (a * lax.rsqrt(ms + EPS) * kn_ref[...]).astype(BF16)

    @pl.when(j == 1)
    def _():
        v_ref[...] = acc.astype(BF16)

    @pl.when(j >= 2)
    def _():
        u_ref[...] = acc


def _ctxproj(ctx2, norm_w, shift_c, scale_c, w_halves, kn, widths):
    mc, d = ctx2.shape
    w_top, w_bot = w_halves
    tn = widths[1]
    assert widths[2] == tn and widths[0] % tn == 0 and widths[3] % tn == 0 and widths[4] % tn == 0
    k_tile = widths[0] // tn
    u_tile = (widths[0] + 2 * tn + widths[3]) // tn
    nu = widths[4] // tn
    one = lambda j: (0, 0)
    col = lambda j: (0, jnp.where(j < 2, k_tile + j, u_tile + j - 2))
    return pl.pallas_call(
        _ctxproj_kernel,
        grid=(2 + nu,),
        in_specs=[pl.BlockSpec((mc, d), one), pl.BlockSpec((1, d), one), pl.BlockSpec((1, d), one),
                  pl.BlockSpec((1, d), one),
                  pl.BlockSpec((w_top.shape[0], tn), col), pl.BlockSpec((w_bot.shape[0], tn), col),
                  pl.BlockSpec((1, HEAD_DIM), one)],
        out_specs=[pl.BlockSpec((mc, tn), one), pl.BlockSpec((mc, tn), one),
                   pl.BlockSpec((mc, tn), lambda j: (0, jnp.maximum(j - 2, 0)))],
        out_shape=[jax.ShapeDtypeStruct((mc, tn), BF16), jax.ShapeDtypeStruct((mc, tn), BF16),
                   jax.ShapeDtypeStruct((mc, widths[4]), F32)],
        scratch_shapes=[pltpu.VMEM((mc, d), BF16)],
        compiler_params=_params("arbitrary"),
        name="ctxproj",
    )(ctx2, norm_w.reshape(1, d), shift_c, scale_c, w_top, w_bot, kn.reshape(1, HEAD_DIM))


def _attn_kernel(n_cast, q_ref, k_ref, v_ref, g_ref, *rest):
    cast_in, o_ref, cast_out = rest[:n_cast], rest[n_cast], rest[n_cast + 1:2 * n_cast + 1]
    m_scr, l_scr, acc_scr = rest[2 * n_cast + 1:]
    _side_cast(cast_in, cast_out)
    kv = pl.program_id(3)

    @pl.when(kv == 0)
    def _():
        m_scr[...] = jnp.full(m_scr.shape, -jnp.inf, F32)
        l_scr[...] = jnp.zeros(l_scr.shape, F32)
        acc_scr[...] = jnp.zeros(acc_scr.shape, F32)

    k = k_ref[...]
    v = v_ref[...]
    for h in range(Q_PER_KV):
        qh = q_ref[:, h * HEAD_DIM:(h + 1) * HEAD_DIM]
        s = lax.dot_general(qh, k, (((1,), (1,)), ((), ())), preferred_element_type=F32)
        m_prev = m_scr[h]
        m_new = jnp.maximum(m_prev, jnp.max(s, axis=1, keepdims=True))
        alpha = jnp.exp2(m_prev - m_new)
        p = jnp.exp2(s - m_new[:, :1])
        l_scr[h] = alpha * l_scr[h] + jnp.sum(p, axis=1, keepdims=True)
        acc_scr[h] = alpha * acc_scr[h] + jnp.dot(p.astype(BF16), v, preferred_element_type=F32)
        m_scr[h] = m_new

    @pl.when(kv == pl.num_programs(3) - 1)
    def _():
        for h in range(Q_PER_KV):
            sl = slice(h * HEAD_DIM, (h + 1) * HEAD_DIM)
            o = acc_scr[h] / l_scr[h]
            o_ref[:, sl] = (o * g_ref[:, sl].astype(F32)).astype(o_ref.dtype)


def _attn_noshift_kernel(n_cast, q_ref, k_ref, v_ref, g_ref, *rest):
    cast_in, o_ref, cast_out = rest[:n_cast], rest[n_cast], rest[n_cast + 1:]
    _side_cast(cast_in, cast_out)
    tq = q_ref.shape[0]
    q = jnp.concatenate([q_ref[:, h * HEAD_DIM:(h + 1) * HEAD_DIM] for h in range(Q_PER_KV)], axis=0)
    s = lax.dot_general(q, k_ref[...], (((1,), (1,)), ((), ())), preferred_element_type=F32)
    p = jnp.exp2(s).astype(BF16)
    v = v_ref[...]
    v_ext = jnp.concatenate([v, jnp.ones_like(v)], axis=1)
    a = jnp.dot(p, v_ext, preferred_element_type=F32)
    for h in range(Q_PER_KV):
        sl = slice(h * HEAD_DIM, (h + 1) * HEAD_DIM)
        ah = a[h * tq:(h + 1) * tq]
        o = ah[:, :HEAD_DIM] / ah[:, HEAD_DIM:]
        o_ref[:, sl] = (o * g_ref[:, sl].astype(F32)).astype(o_ref.dtype)


def _attention_noshift(q, k_all, v_all, gate, *side_casts, tq):
    b, l, aw = q.shape
    s_len, kvw = k_all.shape[1], k_all.shape[2]
    kvh = kvw // HEAD_DIM
    gw = Q_PER_KV * HEAD_DIM
    nq = l // tq
    assert aw == kvh * gw
    cast_specs = _side_cast_specs(side_casts, b * kvh * nq, lambda bi, h, qi: (bi * kvh + h) * nq + qi)
    return pl.pallas_call(
        functools.partial(_attn_noshift_kernel, len(side_casts)),
        grid=(b, kvh, nq),
        in_specs=[pl.BlockSpec((None, tq, gw), lambda bi, h, qi: (bi, qi, h)),
                  pl.BlockSpec((None, s_len, HEAD_DIM), lambda bi, h, qi: (bi, 0, h)),
                  pl.BlockSpec((None, s_len, HEAD_DIM), lambda bi, h, qi: (bi, 0, h)),
                  pl.BlockSpec((None, tq, gw), lambda bi, h, qi: (bi, qi, h))] + cast_specs,
        out_specs=[pl.BlockSpec((None, tq, gw), lambda bi, h, qi: (bi, qi, h))] + cast_specs,
        out_shape=[jax.ShapeDtypeStruct((b, l, aw), BF16)] + [jax.ShapeDtypeStruct(a.shape, BF16) for a in side_casts],
        compiler_params=_params("arbitrary", "arbitrary", "arbitrary"),
        name="attention_noshift",
    )(q, k_all, v_all, gate, *side_casts)


def _attention(q, k_all, v_all, gate, *side_casts, tq):
    tk = _tile(k_all.shape[1], 1408, LANES)
    b, l, aw = q.shape
    s_len, kvw = k_all.shape[1], k_all.shape[2]
    kvh = kvw // HEAD_DIM
    gw = Q_PER_KV * HEAD_DIM
    nq, nk = l // tq, s_len // tk
    assert aw == kvh * gw
    cast_specs = _side_cast_specs(side_casts, b * kvh * nq * nk,
                                  lambda bi, h, qi, ki: ((bi * kvh + h) * nq + qi) * nk + ki)
    return pl.pallas_call(
        functools.partial(_attn_kernel, len(side_casts)),
        grid=(b, kvh, nq, nk),
        in_specs=[pl.BlockSpec((None, tq, gw), lambda bi, h, qi, ki: (bi, qi, h)),
                  pl.BlockSpec((None, tk, HEAD_DIM), lambda bi, h, qi, ki: (bi, ki, h)),
                  pl.BlockSpec((None, tk, HEAD_DIM), lambda bi, h, qi, ki: (bi, ki, h)),
                  pl.BlockSpec((None, tq, gw), lambda bi, h, qi, ki: (bi, qi, h))] + cast_specs,
        out_specs=[pl.BlockSpec((None, tq, gw), lambda bi, h, qi, ki: (bi, qi, h))] + cast_specs,
        out_shape=[jax.ShapeDtypeStruct((b, l, aw), BF16)] + [jax.ShapeDtypeStruct(a.shape, BF16) for a in side_casts],
        scratch_shapes=[pltpu.VMEM((Q_PER_KV, tq, HEAD_DIM), F32)] * 3,
        compiler_params=_params("arbitrary", "arbitrary", "arbitrary", "arbitrary"),
        name="attention",
    )(q, k_all, v_all, gate, *side_casts)


def _s5_param_kernel(reverse, n_cast, lr_ref, li_ref, ldt_ref, b1_ref, bsw_ref, c1_ref, csw_ref, k0_ref, *rest):
    cast_in = rest[:n_cast]
    w_ref, vt_ref, kt_ref, a_ref = rest[n_cast:n_cast + 4]
    cast_out = rest[n_cast + 4:]
    _side_cast(cast_in, cast_out)
    for gi in range(lr_ref.shape[0]):
        _s5_group_params(reverse, lr_ref.at[gi], li_ref.at[gi], ldt_ref.at[gi], b1_ref.at[gi], bsw_ref.at[gi],
                         c1_ref.at[gi], csw_ref.at[gi], k0_ref.at[gi],
                         w_ref.at[gi], vt_ref.at[gi], kt_ref.at[gi], a_ref.at[gi])


def _s5_group_params(reverse, lr_ref, li_ref, ldt_ref, b1_ref, bsw_ref, c1_ref, csw_ref, k0_ref,
                     w_ref, vt_ref, kt_ref, a_ref):
    t = S5_CHUNK
    lr, li = lr_ref[...], li_ref[...]
    dt = jnp.exp(ldt_ref[...])
    lane = lax.broadcasted_iota(jnp.int32, (1, LANES), 1)
    neg_first = jnp.where(lane < 64, -1.0, 1.0).astype(F32)
    jj = lax.broadcasted_iota(jnp.int32, (S5_POW_ROWS, LANES), 0).astype(F32)
    mag = jnp.exp(lr * dt * jj)
    ang = li * dt * jj
    p_re = mag * jnp.cos(ang)
    p_im = mag * jnp.sin(ang)

    a_re, a_im = p_re[1:2], p_im[1:2]
    den = lr * lr + li * li
    n_re = a_re - 1.0
    coef_re = (n_re * lr + a_im * li) / den
    coef_im = (a_im * lr - n_re * li) / den
    b1 = b1_ref[...]
    b2 = bsw_ref[...] * neg_first
    bb1 = coef_re * b1 + coef_im * b2
    bb2 = coef_re * b2 - coef_im * b1

    c1 = c1_ref[...] * (-neg_first)
    csw = csw_ref[...]

    xs = [None] * t
    for j in range(t):
        x_j = p_re[j:j + 1] * bb1 + p_im[j:j + 1] * bb2
        blk = (t - 1 - j) if not reverse else j
        xs[blk] = x_j
        w_ref[blk * 16:(blk + 1) * 16, 0:LANES] = x_j.astype(w_ref.dtype)
        w_ref[blk * 16:(blk + 1) * 16, LANES:2 * LANES] = pltpu.roll(x_j, LANES // 2, 1).astype(w_ref.dtype)
        m = (j + 1) if not reverse else (t - j)
        vt_ref[j * 16:(j + 1) * 16, :] = (p_re[m:m + 1] * c1 - p_im[m:m + 1] * csw).astype(vt_ref.dtype)
    x_all = jnp.concatenate(xs, axis=0)
    kt = lax.dot_general(x_all, c1, (((1,), (1,)), ((), ())),
                         precision=lax.Precision.HIGHEST, preferred_element_type=F32)
    kt_ref[...] = kt
    lag0 = 0 if reverse else (t - 1) * 16
    kt_ref[lag0:lag0 + 16, :] = kt[lag0:lag0 + 16] + k0_ref[...]

    r = lax.broadcasted_iota(jnp.int32, (SUBLANES, LANES), 0)
    last = SUBLANES - 1

    def table(k, expo, keep):
        e = expo.astype(F32) * float(t)
        mg = jnp.exp(lr * dt * e)
        an = li * dt * e
        a_ref[2 * k * SUBLANES:(2 * k + 1) * SUBLANES, :] = jnp.where(keep, mg * jnp.cos(an), 0.0)
        a_ref[(2 * k + 1) * SUBLANES:(2 * k + 2) * SUBLANES, :] = jnp.where(keep, mg * jnp.sin(an) * neg_first, 0.0)

    for k, d in enumerate((1, 2, 4)):
        table(k, jnp.full_like(r, d), (r <= last - d) if reverse else (r >= d))
    table(3, (last - r) if reverse else r, r >= 0)
    table(4, jnp.full_like(r, SUBLANES), r >= 0)


def _s5_params(prm, reverse, k0, cast_mat, cast_row0, cast_rows):
    lam_re, lam_im, log_dt, b_re, b_im, c_re, c_im = prm
    g, p = lam_re.shape
    i = b_re.shape[2]
    assert 2 * p == LANES and i * S5_CHUNK == 256

    def dup(a):
        return jnp.concatenate([a, a], axis=-1).reshape(g, 1, LANES).astype(F32)

    bt_re, bt_im = jnp.swapaxes(b_re, 1, 2).astype(F32), jnp.swapaxes(b_im, 1, 2).astype(F32)
    b1 = jnp.concatenate([bt_re, bt_im], axis=-1)
    bsw = jnp.concatenate([bt_im, bt_re], axis=-1)
    c1 = jnp.concatenate([c_re, c_im], axis=-1).astype(F32)
    csw = jnp.concatenate([c_im, c_re], axis=-1).astype(F32)
    ldt = jnp.broadcast_to(log_dt.astype(F32)[:, None, None], (g, 1, LANES))
    gb = 8 if g % 8 == 0 else 1

    def spec(rows, cols):
        return pl.BlockSpec((gb, rows, cols), lambda gi: (gi, 0, 0))

    steps = g // gb
    slab = CAST_ROWS * -(-cast_rows // (CAST_ROWS * steps))
    nblk = cast_rows // slab
    first = cast_row0 // slab
    assert cast_rows % slab == 0 and cast_row0 % slab == 0 and nblk <= steps
    ncol = cast_mat.shape[1]
    return pl.pallas_call(
        functools.partial(_s5_param_kernel, reverse, 1),
        grid=(steps,),
        in_specs=[spec(1, LANES)] * 3 + [spec(i, LANES)] * 4 + [spec(i, i)]
        + [pl.BlockSpec((slab, ncol), lambda gi: (first + jnp.minimum(gi, nblk - 1), 0))],
        out_specs=[spec(256, 2 * LANES), spec(256, LANES), spec(256, i), spec(S5_TAB_ROWS, LANES),
                   pl.BlockSpec((slab, ncol), lambda gi: (jnp.minimum(gi, nblk - 1), 0))],
        out_shape=[jax.ShapeDtypeStruct((g, 256, 2 * LANES), BF16),
                   jax.ShapeDtypeStruct((g, 256, LANES), BF16),
                   jax.ShapeDtypeStruct((g, 256, i), F32),
                   jax.ShapeDtypeStruct((g, S5_TAB_ROWS, LANES), F32),
                   jax.ShapeDtypeStruct((cast_rows, ncol), BF16)],
        compiler_params=_params("arbitrary"),
        name="s5_params",
    )(dup(lam_re), dup(lam_im), ldt, b1, bsw, c1, csw, k0, cast_mat)


def _toeplitz(kt_f, kt_b):
    t = S5_CHUNK
    g, rows, i = kt_f.shape
    z = jnp.concatenate([kt_f[:, :rows - i], kt_b], axis=1)
    cols = [z[:, (t - 1 - tt) * i:(t - 1 - tt) * i + rows] for tt in range(t)]
    return jnp.stack(cols, axis=2).reshape(g, rows, t * i)


def _slot_transpose8(vs, slot):
    vs = list(vs)
    for d in (4, 2, 1):
        keep = (slot & d) == 0
        new = list(vs)
        for r in range(8):
            if r & d == 0:
                a, b = vs[r], vs[r + d]
                new[r] = jnp.where(keep, a, pltpu.roll(b, 16 * d, 1))
                new[r + d] = jnp.where(keep, pltpu.roll(a, LANES - 16 * d, 1), b)
        vs = new
    return vs


def _gather_chunks(src_ref, dst_scr, n, rb):
    t = S5_CHUNK
    for r0 in range(0, n, rb):
        slot = lax.broadcasted_iota(jnp.int32, (rb, LANES), 1) // 16
        for hi in range(2):
            pieces = [src_ref[pl.ds(r0 * t + hi * 8 + r, rb, stride=t), :] for r in range(8)]
            outs = _slot_transpose8(pieces, slot)
            for gi in range(8):
                dst_scr[gi, r0:r0 + rb, hi * LANES:(hi + 1) * LANES] = outs[gi].astype(BF16)


def _s5_kernel(n, nc, rb, n_cast, ux_ref, uc_ref, d_ref, toep_ref, wf_ref, wb_ref, vf_ref, vb_ref, af_ref, ab_ref,
               *rest):
    cast_in, y_ref, cast_out = rest[:n_cast], rest[n_cast], rest[n_cast + 1:2 * n_cast + 1]
    u_scr, ucx_scr, wf_scr, wfs_scr, wb_scr, wbs_scr, hf_scr, gb_scr = rest[2 * n_cast + 1:]
    _side_cast(cast_in, cast_out)
    t = S5_CHUNK
    ntot = n + nc
    nblk = ntot // SUBLANES
    width = 8 * LANES
    _gather_chunks(ux_ref, u_scr, n, rb)
    _gather_chunks(uc_ref, ucx_scr, nc, min(rb, nc))

    for gi in range(8):
        ls = slice(gi * LANES, (gi + 1) * LANES)
        ug, ucg = u_scr[gi], ucx_scr[gi]
        for w_ref, x_rows, c_rows, dst, dst_s in ((wf_ref, slice(nc, ntot), slice(0, nc), wf_scr, wfs_scr),
                                                  (wb_ref, slice(0, n), slice(n, ntot), wb_scr, wbs_scr)):
            w_x = jnp.dot(ug, w_ref[gi], preferred_element_type=F32)
            w_c = jnp.dot(ucg, w_ref[gi], preferred_element_type=F32)
            dst[x_rows, ls], dst_s[x_rows, ls] = w_x[:, :LANES], w_x[:, LANES:]
            dst[c_rows, ls], dst_s[c_rows, ls] = w_c[:, :LANES], w_c[:, LANES:]

    def tables(a_ref):
        return [jnp.concatenate([a_ref[gi, k * SUBLANES:(k + 1) * SUBLANES, :] for gi in range(8)], axis=1)
                for k in range(S5_TAB_ROWS // SUBLANES)]

    tab_f, tab_b = tables(af_ref), tables(ab_ref)
    row = lax.broadcasted_iota(jnp.int32, (SUBLANES, width), 0)

    def block(src, src_s, dst, r0, h, hs, tab, reverse):
        x, xs = src[pl.ds(r0, SUBLANES), :], src_s[pl.ds(r0, SUBLANES), :]
        for k, d in enumerate((1, 2, 4)):
            sh = SUBLANES - d if reverse else d
            sx, sxs = pltpu.roll(x, sh, 0), pltpu.roll(xs, sh, 0)
            x, xs = x + tab[2 * k] * sx + tab[2 * k + 1] * sxs, xs + tab[2 * k] * sxs - tab[2 * k + 1] * sx
        before = pltpu.roll(x, SUBLANES - 1 if reverse else 1, 0)
        before = jnp.where(row == (SUBLANES - 1 if reverse else 0), 0.0, before)
        hb, hsb = jnp.broadcast_to(h, x.shape), jnp.broadcast_to(hs, x.shape)
        dst[pl.ds(r0, SUBLANES), :] = before + tab[6] * hb + tab[7] * hsb
        e = 0 if reverse else SUBLANES - 1
        q1, q2 = tab[8][0:1], tab[9][0:1]
        return x[e:e + 1] + q1 * h + q2 * hs, xs[e:e + 1] + q1 * hs - q2 * h

    def step(i, carry):
        h, hs, g, gs = carry
        h, hs = block(wf_scr, wfs_scr, hf_scr, pl.multiple_of(i * SUBLANES, SUBLANES), h, hs, tab_f, False)
        g, gs = block(wb_scr, wbs_scr, gb_scr, pl.multiple_of((nblk - 1 - i) * SUBLANES, SUBLANES), g, gs,
                      tab_b, True)
        return h, hs, g, gs

    zero = jnp.zeros((1, width), F32)
    lax.fori_loop(0, nblk, step, (zero, zero, zero, zero))

    nt = (((1,), (1,)), ((), ()))
    for r0 in range(0, n, rb):
        slot = lax.broadcasted_iota(jnp.int32, (rb, LANES), 1) // 16
        ys = []
        for gi in range(8):
            ls = slice(gi * LANES, (gi + 1) * LANES)
            y = jnp.dot(u_scr[gi, r0:r0 + rb, :], toep_ref[gi], preferred_element_type=F32)
            y += lax.dot_general(hf_scr[nc + r0:nc + r0 + rb, ls].astype(BF16), vf_ref[gi], nt,
                                 preferred_element_type=F32)
            y += lax.dot_general(gb_scr[r0:r0 + rb, ls].astype(BF16), vb_ref[gi], nt,
                                 preferred_element_type=F32)
            ys.append(y)
        for hi in range(2):
            outs = _slot_transpose8([ys[gi][:, hi * LANES:(hi + 1) * LANES] for gi in range(8)], slot)
            for r in range(8):
                rows = pl.ds(r0 * t + hi * 8 + r, rb, stride=t)
                y_ref[rows, :] = jax.nn.gelu(outs[r] + d_ref[...] * ux_ref[rows, :])


def _s5(u, uc, d_skip, toep, wf2, wb2, vf, vb, af, ab, side_casts=()):
    b, l, c = u.shape
    lc = uc.shape[1]
    t = S5_CHUNK
    n, nc = l // t, lc // t
    assert l % t == 0 and lc % t == 0 and c % LANES == 0 and n % SUBLANES == 0 and nc % SUBLANES == 0
    rb = _tile(n, 128, SUBLANES)
    ntot = n + nc
    gmat = lambda rows, last: pl.BlockSpec((8, rows, last), lambda bi, ci: (ci, 0, 0))
    nci = c // LANES
    cast_specs = _side_cast_specs(side_casts, b * nci, lambda bi, ci: bi * nci + ci)
    out = pl.pallas_call(
        functools.partial(_s5_kernel, n, nc, rb, len(side_casts)),
        grid=(b, nci),
        in_specs=[pl.BlockSpec((None, l, LANES), lambda bi, ci: (bi, 0, ci)),
                  pl.BlockSpec((None, lc, LANES), lambda bi, ci: (bi, 0, ci)),
                  pl.BlockSpec((1, LANES), lambda bi, ci: (0, ci)),
                  gmat(256, 256), gmat(256, 256), gmat(256, 256), gmat(256, LANES), gmat(256, LANES),
                  gmat(S5_TAB_ROWS, LANES), gmat(S5_TAB_ROWS, LANES)] + cast_specs,
        out_specs=[pl.BlockSpec((None, l, LANES), lambda bi, ci: (bi, 0, ci))] + cast_specs,
        out_shape=[jax.ShapeDtypeStruct((b, l, c), F32)] + [jax.ShapeDtypeStruct(a.shape, BF16) for a in side_casts],
        scratch_shapes=[pltpu.VMEM((8, n, 256), BF16),
                        pltpu.VMEM((8, nc, 256), BF16)] + [pltpu.VMEM((ntot, 8 * LANES), F32)] * 6,
        compiler_params=_params("arbitrary", "arbitrary"),
        name="s5_scan",
    )(u, uc, d_skip.reshape(1, c), toep, wf2, wb2, vf, vb, af, ab, *side_casts)
    return out if side_casts else out[0]


def _glu_kernel(v_ref, vj_ref, w_ref, b_ref, g_ref, o_ref, vb_scr):
    @pl.when(pl.program_id(1) == 0)
    def _():
        vb_scr[...] = v_ref[...].astype(BF16)

    acc = jnp.dot(vb_scr[...], w_ref[...], preferred_element_type=F32) + b_ref[...]
    o_ref[...] = (vj_ref[...] * _sigmoid(acc) * g_ref[...].astype(F32)).astype(o_ref.dtype)


def _glu(v, w_bf, b_glu, gate, tm):
    m, c = v.shape
    tn = _tile(c, 1024, LANES)
    return pl.pallas_call(
        _glu_kernel,
        grid=(m // tm, c // tn),
        in_specs=[pl.BlockSpec((tm, c), lambda i, j: (i, 0)),
                  pl.BlockSpec((tm, tn), lambda i, j: (i, j)),
                  pl.BlockSpec((c, tn), lambda i, j: (0, j)),
                  pl.BlockSpec((1, tn), lambda i, j: (0, j)),
                  pl.BlockSpec((tm, tn), lambda i, j: (i, j))],
        out_specs=pl.BlockSpec((tm, tn), lambda i, j: (i, j)),
        out_shape=jax.ShapeDtypeStruct((m, c), BF16),
        scratch_shapes=[pltpu.VMEM((tm, c), BF16)],
        compiler_params=_params("parallel", "arbitrary"),
        name="ssm_glu",
    )(v, v, w_bf, b_glu.reshape(1, c), gate)


def _outproj_kernel(n_cast, a_ref, s_ref, wa_ref, ws_ref, x_ref, g_ref, *rest):
    cast_in, o_ref, cast_out = rest[:n_cast], rest[n_cast], rest[n_cast + 1:]
    _side_cast(cast_in, cast_out)
    acc = jnp.dot(a_ref[...], wa_ref[...], preferred_element_type=F32)
    acc += jnp.dot(s_ref[...], ws_ref[...], preferred_element_type=F32)
    o_ref[...] = x_ref[...] + g_ref[...] * acc


def _outproj(lhs_a, lhs_s, col_s, w_bf, x2, gate, rows_per_batch, tm, name, side_casts=()):
    m = lhs_a.shape[0]
    kh = w_bf.shape[0] // 2
    d = w_bf.shape[1]
    tn = _tile(d, 512 if side_casts else 1024, LANES)
    nj = d // tn
    tpb = rows_per_batch // tm
    assert rows_per_batch % tm == 0 and lhs_a.shape[1] % kh == 0 and lhs_s.shape[1] % kh == 0
    cast_specs = _side_cast_specs(side_casts, (m // tm) * nj, lambda i, j: i * nj + j)
    out = pl.pallas_call(
        functools.partial(_outproj_kernel, len(side_casts)),
        grid=(m // tm, nj),
        in_specs=[pl.BlockSpec((tm, kh), lambda i, j: (i, 0)),
                  pl.BlockSpec((tm, kh), lambda i, j: (i, col_s)),
                  pl.BlockSpec((kh, tn), lambda i, j: (0, j)),
                  pl.BlockSpec((kh, tn), lambda i, j: (1, j)),
                  pl.BlockSpec((tm, tn), lambda i, j: (i, j)),
                  pl.BlockSpec((None, 1, tn), lambda i, j: (i // tpb, 0, j))] + cast_specs,
        out_specs=[pl.BlockSpec((tm, tn), lambda i, j: (i, j))] + cast_specs,
        out_shape=[jax.ShapeDtypeStruct((m, d), F32)] + [jax.ShapeDtypeStruct(a.shape, BF16) for a in side_casts],
        compiler_params=_params("arbitrary", "arbitrary"),
        name=name,
    )(lhs_a, lhs_s, w_bf, w_bf, x2, gate, *side_casts)
    return out if side_casts else out[0]


def _inproj1_kernel(tpb, x_ref, xp_ref, xq_ref, nw_ref, sh_ref, sc_ref, wb_ref, wc_ref, wx_ref, wg_ref,
                    cw_ref, cb_ref, o_ref, xn_scr):
    i = pl.program_id(0)
    tm = x_ref.shape[0]

    @pl.when(pl.program_id(1) == 0)
    def _():
        nw, sh, sc = nw_ref[...], sh_ref[...], sc_ref[...]
        xn_scr[0:HALO, :] = _modulated(xp_ref[...], nw, sh, sc).astype(BF16)
        slab = min(tm, 256)
        for r in range(0, tm, slab):
            xn_scr[HALO + r:HALO + r + slab, :] = _modulated(x_ref[r:r + slab, :], nw, sh, sc).astype(BF16)
        xn_scr[HALO + tm:, :] = _modulated(xq_ref[...], nw, sh, sc).astype(BF16)

    xn_all = xn_scr[...]
    xn = xn_scr[HALO:HALO + tm, :]
    c_gate = jnp.dot(xn_all, wc_ref[...], preferred_element_type=F32)
    xin = jnp.dot(xn_all, wx_ref[...], preferred_element_type=F32)
    b_gate = jnp.dot(xn, wb_ref[...], preferred_element_type=F32)
    g = jnp.dot(xn, wg_ref[...], preferred_element_type=F32)

    y = c_gate * xin
    rows = y.shape[0]
    row = lax.broadcasted_iota(jnp.int32, y.shape, 0)
    lo = jnp.where((i % tpb) == 0, HALO, 0)
    hi = jnp.where((i % tpb) == tpb - 1, HALO + tm, rows)
    y = jnp.where((row >= lo) & (row < hi), y, 0.0)
    y_prev = pltpu.roll(y, 1, 0)[HALO:HALO + tm]
    y_next = pltpu.roll(y, rows - 1, 0)[HALO:HALO + tm]
    conv = cb_ref[...] + y_prev * cw_ref[0:1, :] + y[HALO:HALO + tm] * cw_ref[1:2, :] + y_next * cw_ref[2:3, :]
    o_ref[...] = (b_gate * conv * _silu(g)).astype(o_ref.dtype)


def _inproj1(x2, norm_w, shift, scale, w_bf, conv_w, conv_b, rows_per_batch, tm):
    m, d = x2.shape
    c = w_bf.shape[1] // 4
    tn = _tile(c, 256, LANES)
    nj = c // tn
    tpb = rows_per_batch // tm
    hb = tm // HALO
    nhb = m // HALO
    assert rows_per_batch % tm == 0 and tm % HALO == 0

    def wspec(p):
        return pl.BlockSpec((d, tn), lambda i, j: (0, p * nj + j))

    return pl.pallas_call(
        functools.partial(_inproj1_kernel, tpb),
        grid=(m // tm, nj),
        in_specs=[pl.BlockSpec((tm, d), lambda i, j: (i, 0), pipeline_mode=pl.Buffered(1)),
                  pl.BlockSpec((HALO, d), lambda i, j: (jnp.maximum(i * hb - 1, 0), 0)),
                  pl.BlockSpec((HALO, d), lambda i, j: (jnp.minimum((i + 1) * hb, nhb - 1), 0)),
                  pl.BlockSpec((1, d), lambda i, j: (0, 0)),
                  pl.BlockSpec((None, 1, d), lambda i, j: (i // tpb, 0, 0)),
                  pl.BlockSpec((None, 1, d), lambda i, j: (i // tpb, 0, 0)),
                  wspec(0), wspec(1), wspec(2), wspec(3),
                  pl.BlockSpec((CONV_K, tn), lambda i, j: (0, j)),
                  pl.BlockSpec((1, tn), lambda i, j: (0, j))],
        out_specs=pl.BlockSpec((tm, tn), lambda i, j: (i, j)),
        out_shape=jax.ShapeDtypeStruct((m, c), BF16),
        scratch_shapes=[pltpu.VMEM((tm + 2 * HALO, d), BF16)],
        compiler_params=_params("parallel", "arbitrary"),
        name="inproj1",
    )(x2, x2, x2, norm_w.reshape(1, d), shift, scale, w_bf, w_bf, w_bf, w_bf, conv_w, conv_b.reshape(1, c))


def _rope_tables(n_tok):
    rows = n_tok // GRID_W
    r, col = np.meshgrid(np.arange(rows, dtype=np.float64), np.arange(GRID_W, dtype=np.float64), indexing="ij")
    axis_dim = HEAD_DIM // 2
    inv_freq = ROPE_THETA ** (-np.arange(0, axis_dim, 2, dtype=np.float64) / axis_dim)
    ar = r.reshape(-1)[:, None] * inv_freq
    ac = col.reshape(-1)[:, None] * inv_freq
    cos_t = np.concatenate([np.cos(ar), np.cos(ar), np.cos(ac), np.cos(ac)], axis=1)
    sin_t = np.concatenate([-np.sin(ar), np.sin(ar), -np.sin(ac), np.sin(ac)], axis=1)
    return jnp.asarray(cos_t, F32), jnp.asarray(sin_t, F32)


def kernel(x, c, ctx, c_ctx, l0_norm_w, l0_w_mod, l0_b_mod, l0_w_in, l0_q_norm_w, l0_k_norm_w, l0_fwd_lam_re, l0_fwd_lam_im, l0_fwd_log_dt, l0_fwd_b_re, l0_fwd_b_im, l0_fwd_c_re, l0_fwd_c_im, l0_bwd_lam_re, l0_bwd_lam_im, l0_bwd_log_dt, l0_bwd_b_re, l0_bwd_b_im, l0_bwd_c_re, l0_bwd_c_im, l0_ssm_d, l0_w_glu, l0_b_glu, l0_w_out, l1_norm_w, l1_w_mod, l1_b_mod, l1_w_in, l1_conv_w, l1_conv_b, l1_w_out):
    b, l, d = x.shape
    lc = ctx.shape[1]
    ssm_w = l0_ssm_d.shape[0]
    att_w = l0_w_out.shape[0] - ssm_w
    kv_w = att_w // Q_PER_KV
    widths = (att_w, kv_w, kv_w, att_w, ssm_w, ssm_w)
    assert sum(widths) == l0_w_in.shape[1]

    pad = (-(b + 1)) % 16
    cond = jnp.concatenate([c, c_ctx[None, :], jnp.zeros((pad, d), F32)], axis=0)
    mod0 = _ada_mod(cond, l0_w_mod, l0_b_mod)
    mod1 = _ada_mod(cond, l1_w_mod, l1_b_mod)
    shift0, scale0, gate0 = (mod0[:b, k * d:(k + 1) * d].reshape(b, 1, d) for k in range(3))
    shift0c, scale0c = (mod0[b, k * d:(k + 1) * d].reshape(1, d) for k in range(2))
    shift1, scale1, gate1 = (mod1[:b, k * d:(k + 1) * d].reshape(b, 1, d) for k in range(3))

    x2 = x.reshape(b * l, d)
    ctx2 = ctx.reshape(b * lc, d)
    n_in = l0_fwd_b_re.shape[2]
    fwd = (l0_fwd_lam_re, l0_fwd_lam_im, l0_fwd_log_dt, l0_fwd_b_re, l0_fwd_b_im, l0_fwd_c_re, l0_fwd_c_im)
    bwd = (l0_bwd_lam_re, l0_bwd_lam_im, l0_bwd_log_dt, l0_bwd_b_re, l0_bwd_b_im, l0_bwd_c_re, l0_bwd_c_im)
    assert d % 2 == 0
    wf, vf, ktf, af, w_in0_top = _s5_params(fwd, False, jnp.zeros((ssm_w // n_in, n_in, n_in), F32),
                                            l0_w_in, 0, d // 2)
    wb, vb, ktb, ab, w_in0_bot = _s5_params(bwd, True, ktf[:, -n_in:, :],
                                            l0_w_in, d // 2, d // 2)
    w_in0 = (w_in0_top, w_in0_bot)

    cos_t, sin_t = _rope_tables(l)
    tm = _tile(l, 512, SUBLANES)
    q, k, v, g_att, u, g_ssm = _inproj0(x2, l0_norm_w, shift0, scale0, w_in0, cos_t, sin_t,
                                        l0_q_norm_w, l0_k_norm_w, widths, l, tm)
    kc, vc, uc = _ctxproj(ctx2, l0_norm_w, shift0c, scale0c, w_in0, l0_k_norm_w, widths)

    k_all = jnp.concatenate([kc.reshape(b, lc, kv_w), k.reshape(b, l, kv_w)], axis=1)
    v_all = jnp.concatenate([vc.reshape(b, lc, kv_w), v.reshape(b, l, kv_w)], axis=1)
    score_bound = HEAD_DIM ** 0.5 * jnp.max(jnp.abs(l0_q_norm_w)) * jnp.max(jnp.abs(l0_k_norm_w))
    att, w_in1, w_out1 = lax.cond(score_bound < SCORE_BOUND_NO_SHIFT,
                                  functools.partial(_attention_noshift, tq=_tile(l, 512, SUBLANES)),
                                  functools.partial(_attention, tq=_tile(l, 512, SUBLANES)),
                                  q.reshape(b, l, att_w), k_all, v_all, g_att.reshape(b, l, att_w),
                                  l1_w_in, l1_w_out)

    toep = _toeplitz(ktf, ktb).astype(BF16)
    tmo = _tile(l, 1024, SUBLANES)
    act, w_glu, w_out0 = _s5(u.reshape(b, l, ssm_w), uc.reshape(b, lc, ssm_w), l0_ssm_d, toep,
                             wf, wb, vf, vb, af, ab, side_casts=(l0_w_glu, l0_w_out))
    ssm = _glu(act.reshape(b * l, ssm_w), w_glu, l0_b_glu, g_ssm, tmo)

    assert att_w == ssm_w
    h1 = _outproj(att.reshape(b * l, att_w), ssm, 0, w_out0, x2, gate0, l, tmo, "outproj0")

    mix1 = _inproj1(h1, l1_norm_w, shift1, scale1, w_in1, l1_conv_w, l1_conv_b, l, tmo)
    h2 = _outproj(mix1, mix1, 1, w_out1, h1, gate1, l, tmo, "outproj1")
    return h2.reshape(b, l, d)
```

```python
import functools
import math

import jax
import jax.numpy as jnp
import numpy as np
from jax import lax
from jax.experimental import pallas as pl
from jax.experimental.pallas import tpu as pltpu

F32 = jnp.float32
BF16 = jnp.bfloat16

EPS = 1e-6
HEAD_DIM = 128
Q_PER_KV = 4
GRID_W = 64
ROPE_THETA = 10000.0
CONV_K = 3
Q_SCALE = HEAD_DIM ** -0.5 * math.log2(math.e)
SCORE_BOUND_NO_SHIFT = 40.0

LANES = 128
SUBLANES = 8
HALO = 16
CAST_ROWS = 16
S5_CHUNK = 16
S5_POW_ROWS = 32
S5_TAB_ROWS = 10 * SUBLANES
VMEM_LIMIT_BYTES = 56 * 1024 * 1024


def _params(*semantics):
    return pltpu.CompilerParams(dimension_semantics=semantics, vmem_limit_bytes=VMEM_LIMIT_BYTES)


def _tile(dim, pref, mult):
    if dim <= pref:
        return dim
    t = (pref // mult) * mult
    while t > mult and dim % t:
        t -= mult
    assert dim % t == 0, (dim, pref, mult)
    return t


def _side_cast_specs(mats, steps, step_of):
    specs = []
    for a in mats:
        slab = CAST_ROWS * -(-a.shape[0] // (CAST_ROWS * steps))
        nblk = a.shape[0] // slab
        assert a.shape[0] % slab == 0 and nblk <= steps, (a.shape, steps)
        specs.append(pl.BlockSpec((slab, a.shape[1]),
                                  lambda *idx, nblk=nblk: (jnp.minimum(step_of(*idx), nblk - 1), 0)))
    return specs


def _side_cast(cast_in, cast_out):
    for src, dst in zip(cast_in, cast_out):
        dst[...] = src[...].astype(BF16)


def _sigmoid(x):
    return 1.0 / (1.0 + jnp.exp(-x))


def _silu(x):
    return x * _sigmoid(x)


def _ada_kernel(c_ref, w_ref, b_ref, o_ref):
    s = _silu(c_ref[...]).astype(BF16)
    o_ref[...] = jnp.dot(s, w_ref[...].astype(BF16), preferred_element_type=F32) + b_ref[...]


def _ada_mod(cond, w_mod, b_mod):
    rows, d = cond.shape
    n = w_mod.shape[1]
    tn = _tile(n, 512, LANES)
    return pl.pallas_call(
        _ada_kernel,
        grid=(n // tn,),
        in_specs=[pl.BlockSpec((rows, d), lambda j: (0, 0)),
                  pl.BlockSpec((d, tn), lambda j: (0, j)),
                  pl.BlockSpec((1, tn), lambda j: (0, j))],
        out_specs=pl.BlockSpec((rows, tn), lambda j: (0, j)),
        out_shape=jax.ShapeDtypeStruct((rows, n), F32),
        compiler_params=_params("parallel"),
        name="ada_mod",
    )(cond, w_mod, b_mod.reshape(1, n))


def _modulated(x, nw, shift, scale):
    ms = jnp.mean(x * x, axis=-1, keepdims=True)
    return (x * lax.rsqrt(ms + EPS)) * (nw * (1.0 + scale)) + shift


def _norm_rope_heads(acc, nw, cos, sin, out_scale, o_ref):
    lane = lax.broadcasted_iota(jnp.int32, (acc.shape[0], HEAD_DIM), 1)
    first_half = (lane % 64) < 32
    for h in range(acc.shape[1] // HEAD_DIM):
        a = acc[:, h * HEAD_DIM:(h + 1) * HEAD_DIM]
        ms = jnp.mean(a * a, axis=-1, keepdims=True)
        a = a * lax.rsqrt(ms + EPS) * nw
        swapped = jnp.where(first_half, pltpu.roll(a, HEAD_DIM - 32, 1), pltpu.roll(a, 32, 1))
        o_ref[:, h * HEAD_DIM:(h + 1) * HEAD_DIM] = ((a * cos + swapped * sin) * out_scale).astype(o_ref.dtype)


def _split_k_dot(xn_scr, wt_ref, wb_ref):
    kh = wt_ref.shape[0]
    acc = jnp.dot(xn_scr[:, :kh], wt_ref[...], preferred_element_type=F32)
    return acc + jnp.dot(xn_scr[:, kh:], wb_ref[...], preferred_element_type=F32)


def _inproj0_kernel(bounds, x_ref, nw_ref, sh_ref, sc_ref, wt_ref, wb_ref, cos_ref, sin_ref, qn_ref, kn_ref,
                    q_ref, k_ref, v_ref, ga_ref, u_ref, gs_ref, xn_scr, acc_scr):
    j = pl.program_id(1)
    e_q, e_k, e_v, e_ga, e_u = bounds

    def dot():
        return _split_k_dot(xn_scr, wt_ref, wb_ref)

    @pl.when(j == 0)
    def _():
        xn_scr[...] = _modulated(x_ref[...], nw_ref[...], sh_ref[...], sc_ref[...]).astype(BF16)
        acc_scr[0] = dot()

    @pl.when((j >= 1) & (j <= e_q))
    def _():
        _norm_rope_heads(acc_scr[(j - 1) % 2], qn_ref[...], cos_ref[...], sin_ref[...], Q_SCALE, q_ref)
        acc_scr[j % 2] = dot()

    @pl.when(j == e_k)
    def _():
        _norm_rope_heads(acc_scr[(j - 1) % 2], kn_ref[...], cos_ref[...], sin_ref[...], 1.0, k_ref)
        v_ref[...] = dot().astype(BF16)

    @pl.when((j >= e_v) & (j < e_ga))
    def _():
        ga_ref[...] = _silu(dot()).astype(BF16)

    @pl.when((j >= e_ga) & (j < e_u))
    def _():
        u_ref[...] = dot()

    @pl.when(j >= e_u)
    def _():
        gs_ref[...] = _silu(dot()).astype(BF16)


def _inproj0(x2, norm_w, shift, scale, w_halves, cos_t, sin_t, qn, kn, widths, rows_per_batch, tm):
    m, d = x2.shape
    w_top, w_bot = w_halves
    n = w_top.shape[1]
    tn = _tile(min(widths), 512, LANES)
    assert all(w % tn == 0 for w in widths) and sum(widths) == n and w_top.shape[0] + w_bot.shape[0] == d
    ends, off = [], 0
    for w in widths:
        off += w
        ends.append(off // tn)
    starts = [0] + ends[:-1]
    assert ends[1] - starts[1] == 1 and ends[2] - starts[2] == 1, "k and v must be one column tile each"
    tpb = rows_per_batch // tm
    assert rows_per_batch % tm == 0 and m % tm == 0

    def out_spec(p):
        lo, hi = starts[p], ends[p]
        late = 1 if p == 0 else 0
        return pl.BlockSpec((tm, tn), lambda i, j: (i, jnp.clip(j - late - lo, 0, hi - lo - 1)))

    dts = (BF16, BF16, BF16, BF16, F32, BF16)
    return pl.pallas_call(
        functools.partial(_inproj0_kernel, tuple(ends[:5])),
        grid=(m // tm, n // tn),
        in_specs=[pl.BlockSpec((tm, d), lambda i, j: (i, 0)),
                  pl.BlockSpec((1, d), lambda i, j: (0, 0)),
                  pl.BlockSpec((None, 1, d), lambda i, j: (i // tpb, 0, 0)),
                  pl.BlockSpec((None, 1, d), lambda i, j: (i // tpb, 0, 0)),
                  pl.BlockSpec((w_top.shape[0], tn), lambda i, j: (0, j)),
                  pl.BlockSpec((w_bot.shape[0], tn), lambda i, j: (0, j)),
                  pl.BlockSpec((tm, HEAD_DIM), lambda i, j: (i % tpb, 0)),
                  pl.BlockSpec((tm, HEAD_DIM), lambda i, j: (i % tpb, 0)),
                  pl.BlockSpec((1, HEAD_DIM), lambda i, j: (0, 0)),
                  pl.BlockSpec((1, HEAD_DIM), lambda i, j: (0, 0))],
        out_specs=[out_spec(p) for p in range(6)],
        out_shape=[jax.ShapeDtypeStruct((m, w), dt) for w, dt in zip(widths, dts)],
        scratch_shapes=[pltpu.VMEM((tm, d), BF16), pltpu.VMEM((2, tm, tn), F32)],
        compiler_params=_params("parallel", "arbitrary"),
        name="inproj0",
    )(x2, norm_w.reshape(1, d), shift, scale, w_top, w_bot, cos_t, sin_t,
      qn.reshape(1, HEAD_DIM), kn.reshape(1, HEAD_DIM))


def _ctxproj_kernel(x_ref, nw_ref, sh_ref, sc_ref, wt_ref, wb_ref, kn_ref, k_ref, v_ref, u_ref, xn_scr):
    j = pl.program_id(0)

    @pl.when(j == 0)
    def _():
        xn_scr[...] = _modulated(x_ref[...], nw_ref[...], sh_ref[...], sc_ref[...]).astype(BF16)

    acc = _split_k_dot(xn_scr, wt_ref, wb_ref)

    @pl.when(j == 0)
    def _():
        for h in range(acc.shape[1] // HEAD_DIM):
            a = acc[:, h * HEAD_DIM:(h + 1) * HEAD_DIM]
            ms = jnp.mean(a * a, axis=-1, keepdims=True)
            k_ref[:, h * HEAD_DIM:(h + 1) * HEAD_DIM] = (a * lax.rsqrt(ms + EPS) * kn_ref[...]).astype(BF16)

    @pl.when(j == 1)
    def _():
        v_ref[...] = acc.astype(BF16)

    @pl.when(j >= 2)
    def _():
        u_ref[...] = acc


def _ctxproj(ctx2, norm_w, shift_c, scale_c, w_halves, kn, widths):
    mc, d = ctx2.shape
    w_top, w_bot = w_halves
    tn = widths[1]
    assert widths[2] == tn and widths[0] % tn == 0 and widths[3] % tn == 0 and widths[4] % tn == 0
    k_tile = widths[0] // tn
    u_tile = (widths[0] + 2 * tn + widths[3]) // tn
    nu = widths[4] // tn
    one = lambda j: (0, 0)
    col = lambda j: (0, jnp.where(j < 2, k_tile + j, u_tile + j - 2))
    return pl.pallas_call(
        _ctxproj_kernel,
        grid=(2 + nu,),
        in_specs=[pl.BlockSpec((mc, d), one), pl.BlockSpec((1, d), one), pl.BlockSpec((1, d), one),
                  pl.BlockSpec((1, d), one),
                  pl.BlockSpec((w_top.shape[0], tn), col), pl.BlockSpec((w_bot.shape[0], tn), col),
                  pl.BlockSpec((1, HEAD_DIM), one)],
        out_specs=[pl.BlockSpec((mc, tn), one), pl.BlockSpec((mc, tn), one),
                   pl.BlockSpec((mc, tn), lambda j: (0, jnp.maximum(j - 2, 0)))],
        out_shape=[jax.ShapeDtypeStruct((mc, tn), BF16), jax.ShapeDtypeStruct((mc, tn), BF16),
                   jax.ShapeDtypeStruct((mc, widths[4]), F32)],
        scratch_shapes=[pltpu.VMEM((mc, d), BF16)],
        compiler_params=_params("arbitrary"),
        name="ctxproj",
    )(ctx2, norm_w.reshape(1, d), shift_c, scale_c, w_top, w_bot, kn.reshape(1, HEAD_DIM))


def _attn_kernel(n_cast, q_ref, k_ref, v_ref, g_ref, *rest):
    cast_in, o_ref, cast_out = rest[:n_cast], rest[n_cast], rest[n_cast + 1:2 * n_cast + 1]
    m_scr, l_scr, acc_scr = rest[2 * n_cast + 1:]
    _side_cast(cast_in, cast_out)
    kv = pl.program_id(3)

    @pl.when(kv == 0)
    def _():
        m_scr[...] = jnp.full(m_scr.shape, -jnp.inf, F32)
        l_scr[...] = jnp.zeros(l_scr.shape, F32)
        acc_scr[...] = jnp.zeros(acc_scr.shape, F32)

    k = k_ref[...]
    v = v_ref[...]
    for h in range(Q_PER_KV):
        qh = q_ref[:, h * HEAD_DIM:(h + 1) * HEAD_DIM]
        s = lax.dot_general(qh, k, (((1,), (1,)), ((), ())), preferred_element_type=F32)
        m_prev = m_scr[h]
        m_new = jnp.maximum(m_prev, jnp.max(s, axis=1, keepdims=True))
        alpha = jnp.exp2(m_prev - m_new)
        p = jnp.exp2(s - m_new[:, :1])
        l_scr[h] = alpha * l_scr[h] + jnp.sum(p, axis=1, keepdims=True)
        acc_scr[h] = alpha * acc_scr[h] + jnp.dot(p.astype(BF16), v, preferred_element_type=F32)
        m_scr[h] = m_new

    @pl.when(kv == pl.num_programs(3) - 1)
    def _():
        for h in range(Q_PER_KV):
            sl = slice(h * HEAD_DIM, (h + 1) * HEAD_DIM)
            o = acc_scr[h] / l_scr[h]
            o_ref[:, sl] = (o * g_ref[:, sl].astype(F32)).astype(o_ref.dtype)


def _attn_noshift_kernel(n_cast, q_ref, k_ref, v_ref, g_ref, *rest):
    cast_in, o_ref, cast_out = rest[:n_cast], rest[n_cast], rest[n_cast + 1:]
    _side_cast(cast_in, cast_out)
    tq = q_ref.shape[0]
    q = jnp.concatenate([q_ref[:, h * HEAD_DIM:(h + 1) * HEAD_DIM] for h in range(Q_PER_KV)], axis=0)
    s = lax.dot_general(q, k_ref[...], (((1,), (1,)), ((), ())), preferred_element_type=F32)
    p = jnp.exp2(s).astype(BF16)
    v = v_ref[...]
    v_ext = jnp.concatenate([v, jnp.ones_like(v)], axis=1)
    a = jnp.dot(p, v_ext, preferred_element_type=F32)
    for h in range(Q_PER_KV):
        sl = slice(h * HEAD_DIM, (h + 1) * HEAD_DIM)
        ah = a[h * tq:(h + 1) * tq]
        o = ah[:, :HEAD_DIM] / ah[:, HEAD_DIM:]
        o_ref[:, sl] = (o * g_ref[:, sl].astype(F32)).astype(o_ref.dtype)


def _attention_noshift(q, k_all, v_all, gate, *side_casts, tq):
    b, l, aw = q.shape
    s_len, kvw = k_all.shape[1], k_all.shape[2]
    kvh = kvw // HEAD_DIM
    gw = Q_PER_KV * HEAD_DIM
    nq = l // tq
    assert aw == kvh * gw
    cast_specs = _side_cast_specs(side_casts, b * kvh * nq, lambda bi, h, qi: (bi * kvh + h) * nq + qi)
    return pl.pallas_call(
        functools.partial(_attn_noshift_kernel, len(side_casts)),
        grid=(b, kvh, nq),
        in_specs=[pl.BlockSpec((None, tq, gw), lambda bi, h, qi: (bi, qi, h)),
                  pl.BlockSpec((None, s_len, HEAD_DIM), lambda bi, h, qi: (bi, 0, h)),
                  pl.BlockSpec((None, s_len, HEAD_DIM), lambda bi, h, qi: (bi, 0, h)),
                  pl.BlockSpec((None, tq, gw), lambda bi, h, qi: (bi, qi, h))] + cast_specs,
        out_specs=[pl.BlockSpec((None, tq, gw), lambda bi, h, qi: (bi, qi, h))] + cast_specs,
        out_shape=[jax.ShapeDtypeStruct((b, l, aw), BF16)] + [jax.ShapeDtypeStruct(a.shape, BF16) for a in side_casts],
        compiler_params=_params("arbitrary", "arbitrary", "arbitrary"),
        name="attention_noshift",
    )(q, k_all, v_all, gate, *side_casts)


def _attention(q, k_all, v_all, gate, *side_casts, tq):
    tk = _tile(k_all.shape[1], 1408, LANES)
    b, l, aw = q.shape
    s_len, kvw = k_all.shape[1], k_all.shape[2]
    kvh = kvw // HEAD_DIM
    gw = Q_PER_KV * HEAD_DIM
    nq, nk = l // tq, s_len // tk
    assert aw == kvh * gw
    cast_specs = _side_cast_specs(side_casts, b * kvh * nq * nk,
                                  lambda bi, h, qi, ki: ((bi * kvh + h) * nq + qi) * nk + ki)
    return pl.pallas_call(
        functools.partial(_attn_kernel, len(side_casts)),
        grid=(b, kvh, nq, nk),
        in_specs=[pl.BlockSpec((None, tq, gw), lambda bi, h, qi, ki: (bi, qi, h)),
                  pl.BlockSpec((None, tk, HEAD_DIM), lambda bi, h, qi, ki: (bi, ki, h)),
                  pl.BlockSpec((None, tk, HEAD_DIM), lambda bi, h, qi, ki: (bi, ki, h)),
                  pl.BlockSpec((None, tq, gw), lambda bi, h, qi, ki: (bi, qi, h))] + cast_specs,
        out_specs=[pl.BlockSpec((None, tq, gw), lambda bi, h, qi, ki: (bi, qi, h))] + cast_specs,
        out_shape=[jax.ShapeDtypeStruct((b, l, aw), BF16)] + [jax.ShapeDtypeStruct(a.shape, BF16) for a in side_casts],
        scratch_shapes=[pltpu.VMEM((Q_PER_KV, tq, HEAD_DIM), F32)] * 3,
        compiler_params=_params("arbitrary", "arbitrary", "arbitrary", "arbitrary"),
        name="attention",
    )(q, k_all, v_all, gate, *side_casts)


def _s5_param_kernel(reverse, n_cast, lr_ref, li_ref, ldt_ref, b1_ref, bsw_ref, c1_ref, csw_ref, k0_ref, *rest):
    cast_in = rest[:n_cast]
    w_ref, vt_ref, kt_ref, a_ref = rest[n_cast:n_cast + 4]
    cast_out = rest[n_cast + 4:]
    _side_cast(cast_in, cast_out)
    for gi in range(lr_ref.shape[0]):
        _s5_group_params(reverse, lr_ref.at[gi], li_ref.at[gi], ldt_ref.at[gi], b1_ref.at[gi], bsw_ref.at[gi],
                         c1_ref.at[gi], csw_ref.at[gi], k0_ref.at[gi],
                         w_ref.at[gi], vt_ref.at[gi], kt_ref.at[gi], a_ref.at[gi])


def _s5_group_params(reverse, lr_ref, li_ref, ldt_ref, b1_ref, bsw_ref, c1_ref, csw_ref, k0_ref,
                     w_ref, vt_ref, kt_ref, a_ref):
    t = S5_CHUNK
    lr, li = lr_ref[...], li_ref[...]
    dt = jnp.exp(ldt_ref[...])
    lane = lax.broadcasted_iota(jnp.int32, (1, LANES), 1)
    neg_first = jnp.where(lane < 64, -1.0, 1.0).astype(F32)
    jj = lax.broadcasted_iota(jnp.int32, (S5_POW_ROWS, LANES), 0).astype(F32)
    mag = jnp.exp(lr * dt * jj)
    ang = li * dt * jj
    p_re = mag * jnp.cos(ang)
    p_im = mag * jnp.sin(ang)

    a_re, a_im = p_re[1:2], p_im[1:2]
    den = lr * lr + li * li
    n_re = a_re - 1.0
    coef_re = (n_re * lr + a_im * li) / den
    coef_im = (a_im * lr - n_re * li) / den
    b1 = b1_ref[...]
    b2 = bsw_ref[...] * neg_first
    bb1 = coef_re * b1 + coef_im * b2
    bb2 = coef_re * b2 - coef_im * b1

    c1 = c1_ref[...] * (-neg_first)
    csw = csw_ref[...]

    xs = [None] * t
    for j in range(t):
        x_j = p_re[j:j + 1] * bb1 + p_im[j:j + 1] * bb2
        blk = (t - 1 - j) if not reverse else j
        xs[blk] = x_j
        w_ref[blk * 16:(blk + 1) * 16, 0:LANES] = x_j.astype(w_ref.dtype)
        w_ref[blk * 16:(blk + 1) * 16, LANES:2 * LANES] = pltpu.roll(x_j, LANES // 2, 1).astype(w_ref.dtype)
        m = (j + 1) if not reverse else (t - j)
        vt_ref[j * 16:(j + 1) * 16, :] = (p_re[m:m + 1] * c1 - p_im[m:m + 1] * csw).astype(vt_ref.dtype)
    x_all = jnp.concatenate(xs, axis=0)
    kt = lax.dot_general(x_all, c1, (((1,), (1,)), ((), ())),
                         precision=lax.Precision.HIGHEST, preferred_element_type=F32)
    kt_ref[...] = kt
    lag0 = 0 if reverse else (t - 1) * 16
    kt_ref[lag0:lag0 + 16, :] = kt[lag0:lag0 + 16] + k0_ref[...]

    r = lax.broadcasted_iota(jnp.int32, (SUBLANES, LANES), 0)
    last = SUBLANES - 1

    def table(k, expo, keep):
        e = expo.astype(F32) * float(t)
        mg = jnp.exp(lr * dt * e)
        an = li * dt * e
        a_ref[2 * k * SUBLANES:(2 * k + 1) * SUBLANES, :] = jnp.where(keep, mg * jnp.cos(an), 0.0)
        a_ref[(2 * k + 1) * SUBLANES:(2 * k + 2) * SUBLANES, :] = jnp.where(keep, mg * jnp.sin(an) * neg_first, 0.0)

    for k, d in enumerate((1, 2, 4)):
        table(k, jnp.full_like(r, d), (r <= last - d) if reverse else (r >= d))
    table(3, (last - r) if reverse else r, r >= 0)
    table(4, jnp.full_like(r, SUBLANES), r >= 0)


def _s5_params(prm, reverse, k0, cast_mat, cast_row0, cast_rows):
    lam_re, lam_im, log_dt, b_re, b_im, c_re, c_im = prm
    g, p = lam_re.shape
    i = b_re.shape[2]
    assert 2 * p == LANES and i * S5_CHUNK == 256

    def dup(a):
        return jnp.concatenate([a, a], axis=-1).reshape(g, 1, LANES).astype(F32)

    bt_re, bt_im = jnp.swapaxes(b_re, 1, 2).astype(F32), jnp.swapaxes(b_im, 1, 2).astype(F32)
    b1 = jnp.concatenate([bt_re, bt_im], axis=-1)
    bsw = jnp.concatenate([bt_im, bt_re], axis=-1)
    c1 = jnp.concatenate([c_re, c_im], axis=-1).astype(F32)
    csw = jnp.concatenate([c_im, c_re], axis=-1).astype(F32)
    ldt = jnp.broadcast_to(log_dt.astype(F32)[:, None, None], (g, 1, LANES))
    gb = 8 if g % 8 == 0 else 1

    def spec(rows, cols):
        return pl.BlockSpec((gb, rows, cols), lambda gi: (gi, 0, 0))

    steps = g // gb
    slab = CAST_ROWS * -(-cast_rows // (CAST_ROWS * steps))
    nblk = cast_rows // slab
    first = cast_row0 // slab
    assert cast_rows % slab == 0 and cast_row0 % slab == 0 and nblk <= steps
    ncol = cast_mat.shape[1]
    return pl.pallas_call(
        functools.partial(_s5_param_kernel, reverse, 1),
        grid=(steps,),
        in_specs=[spec(1, LANES)] * 3 + [spec(i, LANES)] * 4 + [spec(i, i)]
        + [pl.BlockSpec((slab, ncol), lambda gi: (first + jnp.minimum(gi, nblk - 1), 0))],
        out_specs=[spec(256, 2 * LANES), spec(256, LANES), spec(256, i), spec(S5_TAB_ROWS, LANES),
                   pl.BlockSpec((slab, ncol), lambda gi: (jnp.minimum(gi, nblk - 1), 0))],
        out_shape=[jax.ShapeDtypeStruct((g, 256, 2 * LANES), BF16),
                   jax.ShapeDtypeStruct((g, 256, LANES), BF16),
                   jax.ShapeDtypeStruct((g, 256, i), F32),
                   jax.ShapeDtypeStruct((g, S5_TAB_ROWS, LANES), F32),
                   jax.ShapeDtypeStruct((cast_rows, ncol), BF16)],
        compiler_params=_params("arbitrary"),
        name="s5_params",
    )(dup(lam_re), dup(lam_im), ldt, b1, bsw, c1, csw, k0, cast_mat)


def _toeplitz(kt_f, kt_b):
    t = S5_CHUNK
    g, rows, i = kt_f.shape
    z = jnp.concatenate([kt_f[:, :rows - i], kt_b], axis=1)
    cols = [z[:, (t - 1 - tt) * i:(t - 1 - tt) * i + rows] for tt in range(t)]
    return jnp.stack(cols, axis=2).reshape(g, rows, t * i)


def _slot_transpose8(vs, slot):
    vs = list(vs)
    for d in (4, 2, 1):
        keep = (slot & d) == 0
        new = list(vs)
        for r in range(8):
            if r & d == 0:
                a, b = vs[r], vs[r + d]
                new[r] = jnp.where(keep, a, pltpu.roll(b, 16 * d, 1))
                new[r + d] = jnp.where(keep, pltpu.roll(a, LANES - 16 * d, 1), b)
        vs = new
    return vs


def _gather_chunks(src_ref, dst_scr, n, rb):
    t = S5_CHUNK
    for r0 in range(0, n, rb):
        slot = lax.broadcasted_iota(jnp.int32, (rb, LANES), 1) // 16
        for hi in range(2):
            pieces = [src_ref[pl.ds(r0 * t + hi * 8 + r, rb, stride=t), :] for r in range(8)]
            outs = _slot_transpose8(pieces, slot)
            for gi in range(8):
                dst_scr[gi, r0:r0 + rb, hi * LANES:(hi + 1) * LANES] = outs[gi].astype(BF16)


def _s5_kernel(n, nc, rb, n_cast, ux_ref, uc_ref, d_ref, toep_ref, wf_ref, wb_ref, vf_ref, vb_ref, af_ref, ab_ref,
               *rest):
    cast_in, y_ref, cast_out = rest[:n_cast], rest[n_cast], rest[n_cast + 1:2 * n_cast + 1]
    u_scr, ucx_scr, wf_scr, wfs_scr, wb_scr, wbs_scr, hf_scr, gb_scr = rest[2 * n_cast + 1:]
    _side_cast(cast_in, cast_out)
    t = S5_CHUNK
    ntot = n + nc
    nblk = ntot // SUBLANES
    width = 8 * LANES
    _gather_chunks(ux_ref, u_scr, n, rb)
    _gather_chunks(uc_ref, ucx_scr, nc, min(rb, nc))

    for gi in range(8):
        ls = slice(gi * LANES, (gi + 1) * LANES)
        ug, ucg = u_scr[gi], ucx_scr[gi]
        for w_ref, x_rows, c_rows, dst, dst_s in ((wf_ref, slice(nc, ntot), slice(0, nc), wf_scr, wfs_scr),
                                                  (wb_ref, slice(0, n), slice(n, ntot), wb_scr, wbs_scr)):
            w_x = jnp.dot(ug, w_ref[gi], preferred_element_type=F32)
            w_c = jnp.dot(ucg, w_ref[gi], preferred_element_type=F32)
            dst[x_rows, ls], dst_s[x_rows, ls] = w_x[:, :LANES], w_x[:, LANES:]
            dst[c_rows, ls], dst_s[c_rows, ls] = w_c[:, :LANES], w_c[:, LANES:]

    def tables(a_ref):
        return [jnp.concatenate([a_ref[gi, k * SUBLANES:(k + 1) * SUBLANES, :] for gi in range(8)], axis=1)
                for k in range(S5_TAB_ROWS // SUBLANES)]

    tab_f, tab_b = tables(af_ref), tables(ab_ref)
    row = lax.broadcasted_iota(jnp.int32, (SUBLANES, width), 0)

    def block(src, src_s, dst, r0, h, hs, tab, reverse):
        x, xs = src[pl.ds(r0, SUBLANES), :], src_s[pl.ds(r0, SUBLANES), :]
        for k, d in enumerate((1, 2, 4)):
            sh = SUBLANES - d if reverse else d
            sx, sxs = pltpu.roll(x, sh, 0), pltpu.roll(xs, sh, 0)
            x, xs = x + tab[2 * k] * sx + tab[2 * k + 1] * sxs, xs + tab[2 * k] * sxs - tab[2 * k + 1] * sx
        before = pltpu.roll(x, SUBLANES - 1 if reverse else 1, 0)
        before = jnp.where(row == (SUBLANES - 1 if reverse else 0), 0.0, before)
        hb, hsb = jnp.broadcast_to(h, x.shape), jnp.broadcast_to(hs, x.shape)
        dst[pl.ds(r0, SUBLANES), :] = before + tab[6] * hb + tab[7] * hsb
        e = 0 if reverse else SUBLANES - 1
        q1, q2 = tab[8][0:1], tab[9][0:1]
        return x[e:e + 1] + q1 * h + q2 * hs, xs[e:e + 1] + q1 * hs - q2 * h

    def step(i, carry):
        h, hs, g, gs = carry
        h, hs = block(wf_scr, wfs_scr, hf_scr, pl.multiple_of(i * SUBLANES, SUBLANES), h, hs, tab_f, False)
        g, gs = block(wb_scr, wbs_scr, gb_scr, pl.multiple_of((nblk - 1 - i) * SUBLANES, SUBLANES), g, gs,
                      tab_b, True)
        return h, hs, g, gs

    zero = jnp.zeros((1, width), F32)
    lax.fori_loop(0, nblk, step, (zero, zero, zero, zero))

    nt = (((1,), (1,)), ((), ()))
    for r0 in range(0, n, rb):
        slot = lax.broadcasted_iota(jnp.int32, (rb, LANES), 1) // 16
        ys = []
        for gi in range(8):
            ls = slice(gi * LANES, (gi + 1) * LANES)
            y = jnp.dot(u_scr[gi, r0:r0 + rb, :], toep_ref[gi], preferred_element_type=F32)
            y += lax.dot_general(hf_scr[nc + r0:nc + r0 + rb, ls].astype(BF16), vf_ref[gi], nt,
                                 preferred_element_type=F32)
            y += lax.dot_general(gb_scr[r0:r0 + rb, ls].astype(BF16), vb_ref[gi], nt,
                                 preferred_element_type=F32)
            ys.append(y)
        for hi in range(2):
            outs = _slot_transpose8([ys[gi][:, hi * LANES:(hi + 1) * LANES] for gi in range(8)], slot)
            for r in range(8):
                rows = pl.ds(r0 * t + hi * 8 + r, rb, stride=t)
                y_ref[rows, :] = jax.nn.gelu(outs[r] + d_ref[...] * ux_ref[rows, :])


def _s5(u, uc, d_skip, toep, wf2, wb2, vf, vb, af, ab, side_casts=()):
    b, l, c = u.shape
    lc = uc.shape[1]
    t = S5_CHUNK
    n, nc = l // t, lc // t
    assert l % t == 0 and lc % t == 0 and c % LANES == 0 and n % SUBLANES == 0 and nc % SUBLANES == 0
    rb = _tile(n, 128, SUBLANES)
    ntot = n + nc
    gmat = lambda rows, last: pl.BlockSpec((8, rows, last), lambda bi, ci: (ci, 0, 0))
    nci = c // LANES
    cast_specs = _side_cast_specs(side_casts, b * nci, lambda bi, ci: bi * nci + ci)
    out = pl.pallas_call(
        functools.partial(_s5_kernel, n, nc, rb, len(side_casts)),
        grid=(b, nci),
        in_specs=[pl.BlockSpec((None, l, LANES), lambda bi, ci: (bi, 0, ci)),
                  pl.BlockSpec((None, lc, LANES), lambda bi, ci: (bi, 0, ci)),
                  pl.BlockSpec((1, LANES), lambda bi, ci: (0, ci)),
                  gmat(256, 256), gmat(256, 256), gmat(256, 256), gmat(256, LANES), gmat(256, LANES),
                  gmat(S5_TAB_ROWS, LANES), gmat(S5_TAB_ROWS, LANES)] + cast_specs,
        out_specs=[pl.BlockSpec((None, l, LANES), lambda bi, ci: (bi, 0, ci))] + cast_specs,
        out_shape=[jax.ShapeDtypeStruct((b, l, c), F32)] + [jax.ShapeDtypeStruct(a.shape, BF16) for a in side_casts],
        scratch_shapes=[pltpu.VMEM((8, n, 256), BF16),
                        pltpu.VMEM((8, nc, 256), BF16)] + [pltpu.VMEM((ntot, 8 * LANES), F32)] * 6,
        compiler_params=_params("arbitrary", "arbitrary"),
        name="s5_scan",
    )(u, uc, d_skip.reshape(1, c), toep, wf2, wb2, vf, vb, af, ab, *side_casts)
    return out if side_casts else out[0]


def _glu_kernel(v_ref, vj_ref, w_ref, b_ref, g_ref, o_ref, vb_scr):
    @pl.when(pl.program_id(1) == 0)
    def _():
        vb_scr[...] = v_ref[...].astype(BF16)

    acc = jnp.dot(vb_scr[...], w_ref[...], preferred_element_type=F32) + b_ref[...]
    o_ref[...] = (vj_ref[...] * _sigmoid(acc) * g_ref[...].astype(F32)).astype(o_ref.dtype)


def _glu(v, w_bf, b_glu, gate, tm):
    m, c = v.shape
    tn = _tile(c, 1024, LANES)
    return pl.pallas_call(
        _glu_kernel,
        grid=(m // tm, c // tn),
        in_specs=[pl.BlockSpec((tm, c), lambda i, j: (i, 0)),
                  pl.BlockSpec((tm, tn), lambda i, j: (i, j)),
                  pl.BlockSpec((c, tn), lambda i, j: (0, j)),
                  pl.BlockSpec((1, tn), lambda i, j: (0, j)),
                  pl.BlockSpec((tm, tn), lambda i, j: (i, j))],
        out_specs=pl.BlockSpec((tm, tn), lambda i, j: (i, j)),
        out_shape=jax.ShapeDtypeStruct((m, c), BF16),
        scratch_shapes=[pltpu.VMEM((tm, c), BF16)],
        compiler_params=_params("parallel", "arbitrary"),
        name="ssm_glu",
    )(v, v, w_bf, b_glu.reshape(1, c), gate)


def _outproj_kernel(a_ref, s_ref, wa_ref, ws_ref, x_ref, g_ref, o_ref):
    acc = jnp.dot(a_ref[...], wa_ref[...], preferred_element_type=F32)
    acc += jnp.dot(s_ref[...], ws_ref[...], preferred_element_type=F32)
    o_ref[...] = x_ref[...] + g_ref[...] * acc


def _outproj(lhs_a, lhs_s, col_s, w_bf, x2, gate, rows_per_batch, tm, name):
    m = lhs_a.shape[0]
    kh = w_bf.shape[0] // 2
    d = w_bf.shape[1]
    tn = _tile(d, 1024, LANES)
    tpb = rows_per_batch // tm
    assert rows_per_batch % tm == 0 and lhs_a.shape[1] % kh == 0 and lhs_s.shape[1] % kh == 0
    return pl.pallas_call(
        _outproj_kernel,
        grid=(m // tm, d // tn),
        in_specs=[pl.BlockSpec((tm, kh), lambda i, j: (i, 0)),
                  pl.BlockSpec((tm, kh), lambda i, j: (i, col_s)),
                  pl.BlockSpec((kh, tn), lambda i, j: (0, j)),
                  pl.BlockSpec((kh, tn), lambda i, j: (1, j)),
                  pl.BlockSpec((tm, tn), lambda i, j: (i, j)),
                  pl.BlockSpec((None, 1, tn), lambda i, j: (i // tpb, 0, j))],
        out_specs=pl.BlockSpec((tm, tn), lambda i, j: (i, j)),
        out_shape=jax.ShapeDtypeStruct((m, d), F32),
        compiler_params=_params("parallel", "parallel"),
        name=name,
    )(lhs_a, lhs_s, w_bf, w_bf, x2, gate)


def _inproj1_kernel(tpb, x_ref, xp_ref, xq_ref, nw_ref, sh_ref, sc_ref, wb_ref, wc_ref, wx_ref, wg_ref,
                    cw_ref, cb_ref, o_ref, xn_scr):
    i = pl.program_id(0)
    tm = x_ref.shape[0]

    @pl.when(pl.program_id(1) == 0)
    def _():
        nw, sh, sc = nw_ref[...], sh_ref[...], sc_ref[...]
        xn_scr[0:HALO, :] = _modulated(xp_ref[...], nw, sh, sc).astype(BF16)
        slab = min(tm, 256)
        for r in range(0, tm, slab):
            xn_scr[HALO + r:HALO + r + slab, :] = _modulated(x_ref[r:r + slab, :], nw, sh, sc).astype(BF16)
        xn_scr[HALO + tm:, :] = _modulated(xq_ref[...], nw, sh, sc).astype(BF16)

    xn_all = xn_scr[...]
    xn = xn_scr[HALO:HALO + tm, :]
    c_gate = jnp.dot(xn_all, wc_ref[...], preferred_element_type=F32)
    xin = jnp.dot(xn_all, wx_ref[...], preferred_element_type=F32)
    b_gate = jnp.dot(xn, wb_ref[...], preferred_element_type=F32)
    g = jnp.dot(xn, wg_ref[...], preferred_element_type=F32)

    y = c_gate * xin
    rows = y.shape[0]
    row = lax.broadcasted_iota(jnp.int32, y.shape, 0)
    lo = jnp.where((i % tpb) == 0, HALO, 0)
    hi = jnp.where((i % tpb) == tpb - 1, HALO + tm, rows)
    y = jnp.where((row >= lo) & (row < hi), y, 0.0)
    y_prev = pltpu.roll(y, 1, 0)[HALO:HALO + tm]
    y_next = pltpu.roll(y, rows - 1, 0)[HALO:HALO + tm]
    conv = cb_ref[...] + y_prev * cw_ref[0:1, :] + y[HALO:HALO + tm] * cw_ref[1:2, :] + y_next * cw_ref[2:3, :]
    o_ref[...] = (b_gate * conv * _silu(g)).astype(o_ref.dtype)


def _inproj1(x2, norm_w, shift, scale, w_bf, conv_w, conv_b, rows_per_batch, tm):
    m, d = x2.shape
    c = w_bf.shape[1] // 4
    tn = _tile(c, 256, LANES)
    nj = c // tn
    tpb = rows_per_batch // tm
    hb = tm // HALO
    nhb = m // HALO
    assert rows_per_batch % tm == 0 and tm % HALO == 0

    def wspec(p):
        return pl.BlockSpec((d, tn), lambda i, j: (0, p * nj + j))

    return pl.pallas_call(
        functools.partial(_inproj1_kernel, tpb),
        grid=(m // tm, nj),
        in_specs=[pl.BlockSpec((tm, d), lambda i, j: (i, 0), pipeline_mode=pl.Buffered(1)),
                  pl.BlockSpec((HALO, d), lambda i, j: (jnp.maximum(i * hb - 1, 0), 0)),
                  pl.BlockSpec((HALO, d), lambda i, j: (jnp.minimum((i + 1) * hb, nhb - 1), 0)),
                  pl.BlockSpec((1, d), lambda i, j: (0, 0)),
                  pl.BlockSpec((None, 1, d), lambda i, j: (i // tpb, 0, 0)),
                  pl.BlockSpec((None, 1, d), lambda i, j: (i // tpb, 0, 0)),
                  wspec(0), wspec(1), wspec(2), wspec(3),
                  pl.BlockSpec((CONV_K, tn), lambda i, j: (0, j)),
                  pl.BlockSpec((1, tn), lambda i, j: (0, j))],
        out_specs=pl.BlockSpec((tm, tn), lambda i, j: (i, j)),
        out_shape=jax.ShapeDtypeStruct((m, c), BF16),
        scratch_shapes=[pltpu.VMEM((tm + 2 * HALO, d), BF16)],
        compiler_params=_params("parallel", "arbitrary"),
        name="inproj1",
    )(x2, x2, x2, norm_w.reshape(1, d), shift, scale, w_bf, w_bf, w_bf, w_bf, conv_w, conv_b.reshape(1, c))


def _rope_tables(n_tok):
    rows = n_tok // GRID_W
    r, col = np.meshgrid(np.arange(rows, dtype=np.float64), np.arange(GRID_W, dtype=np.float64), indexing="ij")
    axis_dim = HEAD_DIM // 2
    inv_freq = ROPE_THETA ** (-np.arange(0, axis_dim, 2, dtype=np.float64) / axis_dim)
    ar = r.reshape(-1)[:, None] * inv_freq
    ac = col.reshape(-1)[:, None] * inv_freq
    cos_t = np.concatenate([np.cos(ar), np.cos(ar), np.cos(ac), np.cos(ac)], axis=1)
    sin_t = np.concatenate([-np.sin(ar), np.sin(ar), -np.sin(ac), np.sin(ac)], axis=1)
    return jnp.asarray(cos_t, F32), jnp.asarray(sin_t, F32)


def kernel(x, c, ctx, c_ctx, l0_norm_w, l0_w_mod, l0_b_mod, l0_w_in, l0_q_norm_w, l0_k_norm_w, l0_fwd_lam_re, l0_fwd_lam_im, l0_fwd_log_dt, l0_fwd_b_re, l0_fwd_b_im, l0_fwd_c_re, l0_fwd_c_im, l0_bwd_lam_re, l0_bwd_lam_im, l0_bwd_log_dt, l0_bwd_b_re, l0_bwd_b_im, l0_bwd_c_re, l0_bwd_c_im, l0_ssm_d, l0_w_glu, l0_b_glu, l0_w_out, l1_norm_w, l1_w_mod, l1_b_mod, l1_w_in, l1_conv_w, l1_conv_b, l1_w_out):
    b, l, d = x.shape
    lc = ctx.shape[1]
    ssm_w = l0_ssm_d.shape[0]
    att_w = l0_w_out.shape[0] - ssm_w
    kv_w = att_w // Q_PER_KV
    widths = (att_w, kv_w, kv_w, att_w, ssm_w, ssm_w)
    assert sum(widths) == l0_w_in.shape[1]

    pad = (-(b + 1)) % 16
    cond = jnp.concatenate([c, c_ctx[None, :], jnp.zeros((pad, d), F32)], axis=0)
    mod0 = _ada_mod(cond, l0_w_mod, l0_b_mod)
    mod1 = _ada_mod(cond, l1_w_mod, l1_b_mod)
    shift0, scale0, gate0 = (mod0[:b, k * d:(k + 1) * d].reshape(b, 1, d) for k in range(3))
    shift0c, scale0c = (mod0[b, k * d:(k + 1) * d].reshape(1, d) for k in range(2))
    shift1, scale1, gate1 = (mod1[:b, k * d:(k + 1) * d].reshape(b, 1, d) for k in range(3))

    x2 = x.reshape(b * l, d)
    ctx2 = ctx.reshape(b * lc, d)
    n_in = l0_fwd_b_re.shape[2]
    fwd = (l0_fwd_lam_re, l0_fwd_lam_im, l0_fwd_log_dt, l0_fwd_b_re, l0_fwd_b_im, l0_fwd_c_re, l0_fwd_c_im)
    bwd = (l0_bwd_lam_re, l0_bwd_lam_im, l0_bwd_log_dt, l0_bwd_b_re, l0_bwd_b_im, l0_bwd_c_re, l0_bwd_c_im)
    assert d % 2 == 0
    wf, vf, ktf, af, w_in0_top = _s5_params(fwd, False, jnp.zeros((ssm_w // n_in, n_in, n_in), F32),
                                            l0_w_in, 0, d // 2)
    wb, vb, ktb, ab, w_in0_bot = _s5_params(bwd, True, ktf[:, -n_in:, :],
                                            l0_w_in, d // 2, d // 2)
    w_in0 = (w_in0_top, w_in0_bot)

    cos_t, sin_t = _rope_tables(l)
    tm = _tile(l, 512, SUBLANES)
    q, k, v, g_att, u, g_ssm = _inproj0(x2, l0_norm_w, shift0, scale0, w_in0, cos_t, sin_t,
                                        l0_q_norm_w, l0_k_norm_w, widths, l, tm)
    kc, vc, uc = _ctxproj(ctx2, l0_norm_w, shift0c, scale0c, w_in0, l0_k_norm_w, widths)

    k_all = jnp.concatenate([kc.reshape(b, lc, kv_w), k.reshape(b, l, kv_w)], axis=1)
    v_all = jnp.concatenate([vc.reshape(b, lc, kv_w), v.reshape(b, l, kv_w)], axis=1)
    score_bound = HEAD_DIM ** 0.5 * jnp.max(jnp.abs(l0_q_norm_w)) * jnp.max(jnp.abs(l0_k_norm_w))
    att, w_in1, w_out1 = lax.cond(score_bound < SCORE_BOUND_NO_SHIFT,
                                  functools.partial(_attention_noshift, tq=_tile(l, 512, SUBLANES)),
                                  functools.partial(_attention, tq=_tile(l, 512, SUBLANES)),
                                  q.reshape(b, l, att_w), k_all, v_all, g_att.reshape(b, l, att_w),
                                  l1_w_in, l1_w_out)

    toep = _toeplitz(ktf, ktb).astype(BF16)
    tmo = _tile(l, 1024, SUBLANES)
    act, w_glu, w_out0 = _s5(u.reshape(b, l, ssm_w), uc.reshape(b, lc, ssm_w), l0_ssm_d, toep,
                             wf, wb, vf, vb, af, ab, side_casts=(l0_w_glu, l0_w_out))
    ssm = _glu(act.reshape(b * l, ssm_w), w_glu, l0_b_glu, g_ssm, tmo)

    assert att_w == ssm_w
    h1 = _outproj(att.reshape(b * l, att_w), ssm, 0, w_out0, x2, gate0, l, tmo, "outproj0")

    mix1 = _inproj1(h1, l1_norm_w, shift1, scale1, w_in1, l1_conv_w, l1_conv_b, l, tmo)
    h2 = _outproj(mix1, mix1, 1, w_out1, h1, gate1, l, tmo, "outproj1")
    return h2.reshape(b, l, d)
```
